```python
import functools
import jax, jax.numpy as jnp
from jax import lax
import numpy as np

D_MODEL = 4096
BATCH = 4
SEQ = 2048
DEPTH = 2
DEC_BATCH = 8
DEC_SEQ = 4
PAST_LEN = 16384
PAGE_SIZE = 128

MIX_WIDTH = D_MODEL
RET_HEADS = 8
RET_DV = (MIX_WIDTH // 2) // RET_HEADS
RET_DK = RET_DV // 2
RET_CHUNK = 128
ROPE_BASE = 10000.0
NSA_HEADS = 16
NSA_DH = (MIX_WIDTH // 2) // NSA_HEADS
NSA_KV_HEADS = 4
NSA_HPG = NSA_HEADS // NSA_KV_HEADS
CMP_BLOCK = 32
CMP_STRIDE = 16
CMP_RATIO = CMP_BLOCK // CMP_STRIDE
CMP_HIDDEN = 2 * NSA_DH
SEL_BLOCK = 64
SEL_TOPK = 16
WINDOW = 512
WIN_QBLK = 128
SEL_QBLK = 16
D_FF = ((8 * D_MODEL // 3 + 255) // 256) * 256
N_SUB = 3
RMS_EPS = 1e-6
NEG_INF = -1e30
FORCE_SCORE = 1e9
KV_W = NSA_KV_HEADS * NSA_DH
SPLITS = (RET_HEADS * RET_DK, RET_HEADS * RET_DK, RET_HEADS * RET_DV, RET_HEADS * RET_DV,
          NSA_HEADS * NSA_DH, 2 * KV_W, 2 * KV_W, 2 * KV_W, 3 * NSA_HEADS)
N_IN = sum(SPLITS)

kernel_name = 'hybrid_retention_nsa_macaron_step'


def rmsnorm(x, g):
    xf = x.astype(jnp.float32)
    y = xf * lax.rsqrt(jnp.mean(xf * xf, axis=-1, keepdims=True) + RMS_EPS)
    return y.astype(x.dtype) * g


def modulate(h, shift, scale):
    return h * (1.0 + scale[:, None, :]) + shift[:, None, :]


def swiglu(h, w_gate, w_up, w_down):
    return (jax.nn.silu(h @ w_gate) * (h @ w_up)) @ w_down


def split_projection(h, w_in):
    offs = [int(o) for o in np.cumsum(SPLITS)[:-1]]
    return jnp.split(h @ w_in, offs, axis=-1)


def rotary(x, pos):
    d = x.shape[-1]
    half = d // 2
    inv = jnp.power(ROPE_BASE, -jnp.arange(half, dtype=jnp.float32) * 2.0 / d)
    ang = pos.astype(jnp.float32)[:, None] * inv[None, :]
    cos = jnp.cos(ang)[:, None, :]
    sin = jnp.sin(ang)[:, None, :]
    xf = x.astype(jnp.float32)
    x1, x2 = xf[..., :half], xf[..., half:]
    return jnp.concatenate([x1 * cos - x2 * sin, x1 * sin + x2 * cos], axis=-1).astype(x.dtype)


def retention_scan(q, k, v, s0, chunk):
    b, t, nh, _ = q.shape
    n = t // chunk
    f32 = jnp.float32
    log_g = jnp.log(1.0 - jnp.exp2(-5.0 - jnp.arange(nh, dtype=f32)))
    i = jnp.arange(chunk, dtype=f32)
    diff = i[:, None] - i[None, :]
    dmask = jnp.where(diff >= 0, jnp.exp(log_g[:, None, None] * jnp.maximum(diff, 0.0)), 0.0)
    q_dec = jnp.exp(log_g[None, :] * (i[:, None] + 1.0))
    k_dec = jnp.exp(log_g[None, :] * (chunk - 1.0 - i[:, None]))
    c_dec = jnp.exp(log_g * chunk)

    def blocks(a):
        return a.astype(f32).reshape(b, n, chunk, *a.shape[2:]).swapaxes(0, 1)

    def step(s, inp):
        qc, kc, vc = inp
        att = jnp.einsum('bihd,bjhd->bhij', qc, kc) * dmask
        o = (jnp.einsum('bhij,bjhe->bihe', att, vc)
             + jnp.einsum('bihd,bhde->bihe', qc, s) * q_dec[None, :, :, None])
        s = s * c_dec[None, :, None, None] + jnp.einsum('bjhd,bjhe->bhde', kc * k_dec[None, :, :, None], vc)
        return s, o

    s, o = lax.scan(step, s0.astype(f32), (blocks(q), blocks(k), blocks(v)))
    return o.swapaxes(0, 1).reshape(b, t, nh, v.shape[-1]), s


def retention(rq, rk, rv, rg, pos, s0, chunk):
    b, t, _ = rq.shape
    q = rotary(rq.reshape(b, t, RET_HEADS, RET_DK), pos)
    k = rotary(rk.reshape(b, t, RET_HEADS, RET_DK), pos) * (RET_DK ** -0.5)
    v = rv.reshape(b, t, RET_HEADS, RET_DV)
    o, s = retention_scan(q, k, v, s0, chunk)
    o = o * lax.rsqrt(jnp.mean(o * o, axis=-1, keepdims=True) + RMS_EPS)
    y = jax.nn.silu(rg) * o.reshape(b, t, RET_HEADS * RET_DV).astype(rg.dtype)
    return y, s


def masked_softmax(s, mask):
    s = jnp.where(mask, s, NEG_INF)
    m = jnp.max(s, axis=-1, keepdims=True)
    e = jnp.where(mask, jnp.exp(s - m), 0.0)
    return e / jnp.maximum(jnp.sum(e, axis=-1, keepdims=True), 1e-30)


def attend(q, kv, mask):
    k, v = kv[..., 0, :, :], kv[..., 1, :, :]
    s = jnp.einsum('...qgpd,...kgd->...gpqk', q, k).astype(jnp.float32) * (NSA_DH ** -0.5)
    p = masked_softmax(s, mask[..., None, None, :, :])
    o = jnp.einsum('...gpqk,...kgd->...qgpd', p.astype(v.dtype), v)
    return o, p


def compress(kv, cmp_pe, cmp_w1, cmp_b1, cmp_w2):
    b, t = kv.shape[:2]
    nseg = t // CMP_STRIDE
    nc = nseg - CMP_RATIO + 1
    seg = kv[:, :nseg * CMP_STRIDE].reshape(b, nseg, CMP_STRIDE, 2, NSA_KV_HEADS, NSA_DH)
    w1r = cmp_w1.reshape(2, CMP_RATIO, CMP_STRIDE, NSA_DH, CMP_HIDDEN)
    per = cmp_pe.reshape(2, CMP_RATIO, CMP_STRIDE, NSA_DH)
    hid = cmp_b1[None, None, :, None, :]
    for r in range(CMP_RATIO):
        part = (jnp.einsum('bnscgd,csdh->bncgh', seg, w1r[:, r])
                + jnp.einsum('csd,csdh->ch', per[:, r], w1r[:, r])[None, None, :, None, :])
        hid = hid + part[:, r:r + nc]
    return jnp.einsum('bncgh,chd->bncgd', jax.nn.silu(hid), cmp_w2)


def select_blocks(p_cmp, t_pos, n_sel):
    pg = jnp.sum(p_cmp, axis=2)
    ratio = SEL_BLOCK // CMP_STRIDE
    pre = CMP_RATIO - 1
    span = ratio + CMP_RATIO - 1
    total = n_sel * ratio + span
    nc = pg.shape[-1]
    pp = jnp.pad(pg, [(0, 0), (0, 0), (0, 0), (pre, max(0, total - pre - nc))])
    p_slc = sum(pp[..., s:s + n_sel * ratio:ratio] for s in range(span))
    jb = jnp.arange(n_sel)[None, :]
    jc = (t_pos // SEL_BLOCK)[:, None]
    forced = (jb == 0) | (jb == jc) | (jb == jc - 1)
    score = jnp.where(jb > jc, NEG_INF, jnp.where(forced, FORCE_SCORE, p_slc))
    _, idx = lax.top_k(score, min(SEL_TOPK, n_sel))
    return idx.transpose(0, 2, 1, 3)


def gather_blocks(blocks, idx):
    bi = jnp.arange(blocks.shape[0])[:, None, None, None]
    gi = jnp.arange(NSA_KV_HEADS)[None, None, :, None]
    return blocks[bi, idx, :, :, gi, :]


def gather_blocks_paged(sel_pool, page_table, kv_new, idx, past_len):
    bpp = sel_pool.shape[1] // SEL_BLOCK
    pool_b = sel_pool.reshape(sel_pool.shape[0] * bpp, SEL_BLOCK, 2, NSA_KV_HEADS, NSA_DH)
    nb_past = past_len // SEL_BLOCK
    b, t = kv_new.shape[:2]
    nb_new = -(-t // SEL_BLOCK)
    new_b = jnp.pad(kv_new, [(0, 0), (0, nb_new * SEL_BLOCK - t), (0, 0), (0, 0), (0, 0)])
    new_b = new_b.reshape(b, nb_new, SEL_BLOCK, 2, NSA_KV_HEADS, NSA_DH)
    bi = jnp.arange(b)[:, None, None, None]
    gi = jnp.arange(NSA_KV_HEADS)[None, None, :, None]
    lp = jnp.minimum(idx, nb_past - 1)
    phys = page_table[bi, lp // bpp] * bpp + lp % bpp
    g_past = pool_b[phys, :, :, gi, :].astype(kv_new.dtype)
    g_new = new_b[bi, jnp.clip(idx - nb_past, 0, nb_new - 1), :, :, gi, :]
    return jnp.where((idx < nb_past)[..., None, None, None], g_past, g_new)


def sel_attend(q, kvb, idx, t_pos):
    b, nq, g, p, d = q.shape
    k, v = kvb[..., 0, :], kvb[..., 1, :]
    s = jnp.einsum('bqgpd,bqgkld->bqgpkl', q, k).astype(jnp.float32) * (NSA_DH ** -0.5)
    kpos = idx[..., None] * SEL_BLOCK + jnp.arange(SEL_BLOCK)
    mask = (kpos <= t_pos[None, :, None, None, None]).reshape(b, nq, g, 1, -1)
    pr = masked_softmax(s.reshape(b, nq, g, p, -1), mask)
    return jnp.einsum('bqgpn,bqgnd->bqgpd', pr.astype(v.dtype), v.reshape(b, nq, g, -1, d))


def sel_branch_prompt(q, kv_s, idx, t_pos):
    b, t = q.shape[:2]
    blocks = kv_s.reshape(b, t // SEL_BLOCK, SEL_BLOCK, 2, NSA_KV_HEADS, NSA_DH)
    nq = t // SEL_QBLK
    qs = q.reshape(b, nq, SEL_QBLK, NSA_KV_HEADS, NSA_HPG, NSA_DH).swapaxes(0, 1)
    ids = idx.reshape(b, nq, SEL_QBLK, NSA_KV_HEADS, idx.shape[-1]).swapaxes(0, 1)
    ts = t_pos.reshape(nq, SEL_QBLK)

    def body(args):
        qb, ib, tb = args
        return sel_attend(qb, gather_blocks(blocks, ib), ib, tb)

    o = lax.map(body, (qs, ids, ts))
    return o.swapaxes(0, 1).reshape(b, t, NSA_KV_HEADS, NSA_HPG, NSA_DH)


def window_branch_prompt(q, kv_w, t_pos):
    b, t = q.shape[:2]
    nqb = t // WIN_QBLK
    nback = WINDOW // WIN_QBLK
    kvb = kv_w.reshape(b, nqb, WIN_QBLK, 2, NSA_KV_HEADS, NSA_DH)
    kvp = jnp.pad(kvb, [(0, 0), (nback, 0), (0, 0), (0, 0), (0, 0), (0, 0)])
    band = jnp.concatenate([kvp[:, j:j + nqb] for j in range(nback + 1)], axis=2)
    qb = q.reshape(b, nqb, WIN_QBLK, NSA_KV_HEADS, NSA_HPG, NSA_DH)
    qp = t_pos.reshape(nqb, WIN_QBLK)[:, :, None]
    kp = ((jnp.arange(nqb)[:, None] - nback) * WIN_QBLK + jnp.arange((nback + 1) * WIN_QBLK)[None, :])[:, None, :]
    mask = (kp >= 0) & (kp <= qp) & (kp > qp - WINDOW)
    o, _ = attend(qb, band, mask)
    return o.reshape(b, t, NSA_KV_HEADS, NSA_HPG, NSA_DH)


def nsa_merge(ng, o_c, o_s, o_w):
    b, t, _ = ng.shape
    g = jax.nn.sigmoid(ng.astype(jnp.float32)).reshape(b, t, 3, NSA_KV_HEADS, NSA_HPG, 1).astype(o_c.dtype)
    y = g[:, :, 0] * o_c + g[:, :, 1] * o_s + g[:, :, 2] * o_w
    return y.reshape(b, t, NSA_HEADS * NSA_DH)


def last_rows(a, n):
    t = a.shape[1]
    if t >= n:
        return a[:, t - n:]
    return jnp.pad(a, [(0, 0), (n - t, 0)] + [(0, 0)] * (a.ndim - 2))


def nsa_heads(nq, kv_c, kv_s, kv_w):
    b, t, _ = nq.shape
    kvshape = (b, t, 2, NSA_KV_HEADS, NSA_DH)
    return (nq.reshape(b, t, NSA_KV_HEADS, NSA_HPG, NSA_DH), kv_c.reshape(kvshape),
            kv_s.reshape(kvshape), kv_w.reshape(kvshape))


def mixer_prompt(h, w_in, w_out, cmp_params, win_buf):
    b, t, _ = h.shape
    t_pos = jnp.arange(t, dtype=jnp.int32)
    rq, rk, rv, rg, nq, kv_c, kv_s, kv_w, ng = split_projection(h, w_in)
    s0 = jnp.zeros((b, RET_HEADS, RET_DK, RET_DV), jnp.float32)
    y_ret, s_ret = retention(rq, rk, rv, rg, t_pos, s0, min(RET_CHUNK, t))
    q, kv_c, kv_s, kv_w = nsa_heads(nq, kv_c, kv_s, kv_w)
    kc = compress(kv_c, *cmp_params)
    c_end = jnp.arange(kc.shape[1]) * CMP_STRIDE + CMP_BLOCK - 1
    o_c, p_c = attend(q, kc, c_end[None, :] <= t_pos[:, None])
    idx = select_blocks(p_c, t_pos, t // SEL_BLOCK)
    o_s = sel_branch_prompt(q, kv_s, idx, t_pos)
    o_w = window_branch_prompt(q, kv_w, t_pos)
    y = jnp.concatenate([y_ret, nsa_merge(ng, o_c, o_s, o_w)], axis=-1) @ w_out
    return y, (kv_c, kv_s, last_rows(kv_w, win_buf), s_ret)


def mixer_sample(h, w_in, w_out, cmp_params, cmp_pool, sel_pool, win_kv, s0, page_table, past_len):
    b, t, _ = h.shape
    t_pos = past_len + jnp.arange(t, dtype=jnp.int32)
    rq, rk, rv, rg, nq, kv_c, kv_s, kv_w, ng = split_projection(h, w_in)
    y_ret, s_ret = retention(rq, rk, rv, rg, t_pos, s0, t)
    q, kv_c, kv_s, kv_w = nsa_heads(nq, kv_c, kv_s, kv_w)
    past_c = cmp_pool[page_table].reshape(b, past_len, 2, NSA_KV_HEADS, NSA_DH).astype(kv_c.dtype)
    kc = compress(jnp.concatenate([past_c, kv_c], axis=1), *cmp_params)
    c_end = jnp.arange(kc.shape[1]) * CMP_STRIDE + CMP_BLOCK - 1
    o_c, p_c = attend(q, kc, c_end[None, :] <= t_pos[:, None])
    idx = select_blocks(p_c, t_pos, -(-(past_len + t) // SEL_BLOCK))
    o_s = sel_attend(q, gather_blocks_paged(sel_pool, page_table, kv_s, idx, past_len), idx, t_pos)
    wb = win_kv.shape[1]
    keys = jnp.concatenate([win_kv.astype(kv_w.dtype), kv_w], axis=1)
    kpos = past_len - wb + jnp.arange(wb + t)
    mask = (kpos[None, :] <= t_pos[:, None]) & (kpos[None, :] > t_pos[:, None] - WINDOW)
    o_w, _ = attend(q, keys, mask)
    y = jnp.concatenate([y_ret, nsa_merge(ng, o_c, o_s, o_w)], axis=-1) @ w_out
    return y, (kv_c, kv_s, keys[:, t:], s_ret)


def layer(x, c, mixer, ln_g, w_ada, b_ada, w_gate, w_up, w_down):
    b = x.shape[0]
    mod = (jax.nn.silu(c) @ w_ada + b_ada).reshape(b, N_SUB, 3, D_MODEL)
    h = modulate(rmsnorm(x, ln_g[0]), mod[:, 0, 0], mod[:, 0, 1])
    x = x + 0.5 * mod[:, 0, 2][:, None, :] * swiglu(h, w_gate[0], w_up[0], w_down[0])
    h = modulate(rmsnorm(x, ln_g[1]), mod[:, 1, 0], mod[:, 1, 1])
    out, states = mixer(h)
    x = x + mod[:, 1, 2][:, None, :] * out
    h = modulate(rmsnorm(x, ln_g[2]), mod[:, 2, 0], mod[:, 2, 1])
    x = x + 0.5 * mod[:, 2, 2][:, None, :] * swiglu(h, w_gate[1], w_up[1], w_down[1])
    return x, states


def setup_inputs(seed: int = 0) -> dict:
    key = jax.random.key(seed)
    ks = jax.random.split(key, 22)
    f32 = jnp.float32
    n_pages = PAST_LEN // PAGE_SIZE
    n_pool = (5 * DEC_BATCH * n_pages + 3) // 4
    win_buf = min(WINDOW, PAST_LEN)

    def nrm(k, shape, scale):
        return jax.random.normal(k, shape, f32) * scale

    page_table = jax.random.permutation(ks[7], n_pool)[:DEC_BATCH * n_pages]
    page_table = page_table.reshape(DEC_BATCH, n_pages).astype(jnp.int32)
    return {
        'x_prompt': nrm(ks[0], (BATCH, SEQ, D_MODEL), 1.0),
        'x_sample': nrm(ks[1], (DEC_BATCH, DEC_SEQ, D_MODEL), 1.0),
        'cache_cmp_kv': nrm(ks[2], (DEPTH, n_pool, PAGE_SIZE, 2, NSA_KV_HEADS, NSA_DH), 1.0),
        'cache_sel_kv': nrm(ks[3], (DEPTH, n_pool, PAGE_SIZE, 2, NSA_KV_HEADS, NSA_DH), 1.0),
        'cache_win_kv': nrm(ks[4], (DEPTH, DEC_BATCH, win_buf, 2, NSA_KV_HEADS, NSA_DH), 1.0),
        'state_ret': nrm(ks[5], (DEPTH, DEC_BATCH, RET_HEADS, RET_DK, RET_DV), 1.0),
        'page_table': page_table,
        'c_prompt': nrm(ks[6], (BATCH, D_MODEL), 1.0),
        'c_sample': nrm(ks[8], (DEC_BATCH, D_MODEL), 1.0),
        'norm_g': 1.0 + nrm(ks[9], (DEPTH, N_SUB, D_MODEL), 0.01),
        'w_ada': nrm(ks[10], (DEPTH, D_MODEL, N_SUB * 3 * D_MODEL), 0.5 * D_MODEL ** -0.5),
        'b_ada': nrm(ks[11], (DEPTH, N_SUB * 3 * D_MODEL), 0.01),
        'w_in': nrm(ks[12], (DEPTH, D_MODEL, N_IN), D_MODEL ** -0.5),
        'w_out': nrm(ks[13], (DEPTH, MIX_WIDTH, D_MODEL), MIX_WIDTH ** -0.5),
        'cmp_pe': nrm(ks[14], (DEPTH, 2, CMP_BLOCK, NSA_DH), 0.1),
        'cmp_w1': nrm(ks[15], (DEPTH, 2, CMP_BLOCK, NSA_DH, CMP_HIDDEN), (CMP_BLOCK * NSA_DH) ** -0.5),
        'cmp_b1': nrm(ks[16], (DEPTH, 2, CMP_HIDDEN), 0.01),
        'cmp_w2': nrm(ks[17], (DEPTH, 2, CMP_HIDDEN, NSA_DH), CMP_HIDDEN ** -0.5),
        'ffn_w_gate': nrm(ks[18], (DEPTH, 2, D_MODEL, D_FF), D_MODEL ** -0.5),
        'ffn_w_up': nrm(ks[19], (DEPTH, 2, D_MODEL, D_FF), D_MODEL ** -0.5),
        'ffn_w_down': nrm(ks[20], (DEPTH, 2, D_FF, D_MODEL), D_FF ** -0.5),
        'final_g': 1.0 + nrm(ks[21], (D_MODEL,), 0.01),
    }


def reference(x_prompt, x_sample, cache_cmp_kv, cache_sel_kv, cache_win_kv, state_ret, page_table,
              c_prompt, c_sample, norm_g, w_ada, b_ada, w_in, w_out, cmp_pe, cmp_w1, cmp_b1, cmp_w2,
              ffn_w_gate, ffn_w_up, ffn_w_down, final_g):
    past_len = page_table.shape[1] * cache_cmp_kv.shape[2]
    win_buf = cache_win_kv.shape[2]
    xp, xs = x_prompt, x_sample
    pc, ps, pw, pr, sc, ss, sw, sr = ([] for _ in range(8))
    for l in range(DEPTH):
        cmp_params = (cmp_pe[l], cmp_w1[l], cmp_b1[l], cmp_w2[l])
        ffn = (ffn_w_gate[l], ffn_w_up[l], ffn_w_down[l])
        mix_p = functools.partial(mixer_prompt, w_in=w_in[l], w_out=w_out[l],
                                  cmp_params=cmp_params, win_buf=win_buf)
        xp, (kc, ksel, kw, s) = layer(xp, c_prompt, mix_p, norm_g[l], w_ada[l], b_ada[l], *ffn)
        pc.append(kc); ps.append(ksel); pw.append(kw); pr.append(s)
        mix_s = functools.partial(mixer_sample, w_in=w_in[l], w_out=w_out[l], cmp_params=cmp_params,
                                  cmp_pool=cache_cmp_kv[l], sel_pool=cache_sel_kv[l],
                                  win_kv=cache_win_kv[l], s0=state_ret[l],
                                  page_table=page_table, past_len=past_len)
        xs, (kc, ksel, kw, s) = layer(xs, c_sample, mix_s, norm_g[l], w_ada[l], b_ada[l], *ffn)
        sc.append(kc); ss.append(ksel); sw.append(kw); sr.append(s)
    y_prompt = rmsnorm(xp, final_g)
    y_sample = rmsnorm(xs, final_g)
    return (y_prompt, y_sample, jnp.stack(pc), jnp.stack(ps), jnp.stack(pw), jnp.stack(pr),
            jnp.stack(sc), jnp.stack(ss), jnp.stack(sw), jnp.stack(sr))
```

```python
import functools

import jax
import jax.numpy as jnp
import numpy as np
from jax import lax
from jax.experimental import pallas as pl
from jax.experimental.pallas import tpu as pltpu

F32 = jnp.float32
BF16 = jnp.bfloat16

D_MODEL = 4096
DEPTH = 2
RET_HEADS = 8
RET_DK = 128
RET_DV = 256
RET_CHUNK = 128
ROPE_BASE = 10000.0
NSA_HEADS = 16
NSA_DH = 128
NSA_KV_HEADS = 4
NSA_HPG = NSA_HEADS // NSA_KV_HEADS
CMP_BLOCK = 32
CMP_STRIDE = 16
CMP_RATIO = CMP_BLOCK // CMP_STRIDE
CMP_HIDDEN = 2 * NSA_DH
SEL_BLOCK = 64
SEL_TOPK = 16
WINDOW = 512
N_SUB = 3
RMS_EPS = 1e-6
NEG_INF = -1e30
FORCE_SCORE = 1e9
KV_W = NSA_KV_HEADS * NSA_DH
ATT_SCALE = NSA_DH ** -0.5

COL_RQ = 0
COL_RK = COL_RQ + RET_HEADS * RET_DK
COL_RV = COL_RK + RET_HEADS * RET_DK
COL_RG = COL_RV + RET_HEADS * RET_DV
COL_NQ = COL_RG + RET_HEADS * RET_DV
COL_KVC = COL_NQ + NSA_HEADS * NSA_DH
COL_KVS = COL_KVC + 2 * KV_W
COL_KVW = COL_KVS + 2 * KV_W
COL_NG = COL_KVW + 2 * KV_W
N_PROJ = COL_NG + NSA_KV_HEADS * 128

LANE = 128
SAMPLE_ROWS = 8
VMEM_LIMIT_BYTES = 60 * 1024 * 1024


def _cparams(*sem):
    return pltpu.CompilerParams(dimension_semantics=sem, vmem_limit_bytes=VMEM_LIMIT_BYTES)


def _silu(x):
    return x * jax.nn.sigmoid(x)


def _dot(a, b):
    return jnp.dot(a, b, preferred_element_type=F32)


def _dot_nt(a, b):
    return lax.dot_general(a, b, (((1,), (1,)), ((), ())), preferred_element_type=F32)


def _masked_softmax(s, valid):
    s = jnp.where(valid, s, NEG_INF)
    m = jnp.max(s, axis=-1, keepdims=True)
    e = jnp.where(valid, jnp.exp(s - m), 0.0)
    return e / jnp.maximum(jnp.sum(e, axis=-1, keepdims=True), 1e-30)


def _mod_kernel(c_ref, w_ref, b_ref, o_ref):
    a = _silu(c_ref[...]).astype(BF16)
    o_ref[...] = _dot(a, w_ref[...].astype(BF16)) + b_ref[...]


def _modulation(c_all, w_ada, b_ada):
    rows = c_all.shape[0]
    n = w_ada.shape[-1]
    bn = 512
    return pl.pallas_call(
        _mod_kernel,
        grid=(DEPTH, n // bn),
        in_specs=[pl.BlockSpec((rows, D_MODEL), lambda l, j: (0, 0)),
                  pl.BlockSpec((None, D_MODEL, bn), lambda l, j: (l, 0, j)),
                  pl.BlockSpec((None, 1, bn), lambda l, j: (l, 0, j))],
        out_specs=pl.BlockSpec((None, rows, bn), lambda l, j: (l, 0, j)),
        out_shape=jax.ShapeDtypeStruct((DEPTH, rows, n), F32),
        compiler_params=_cparams("parallel", "parallel"),
    )(c_all, w_ada, b_ada.reshape(DEPTH, 1, n))


def _norm_modulate_into(h_ref, x_ref, g_ref, sh_ref, sc_ref):
    rows = x_ref.shape[0]
    per_row = sh_ref.shape[0] != 1
    step = min(rows, 128)
    for r0 in range(0, rows, step):
        x = x_ref[r0:r0 + step, :]
        y = x * lax.rsqrt(jnp.mean(x * x, axis=-1, keepdims=True) + RMS_EPS) * g_ref[...]
        sc = sc_ref[r0:r0 + step, :] if per_row else sc_ref[...]
        sh = sh_ref[r0:r0 + step, :] if per_row else sh_ref[...]
        h_ref[r0:r0 + step, :] = (y * (1.0 + sc) + sh).astype(h_ref.dtype)


def _mod_spec(mod_rows, bm):
    return pl.BlockSpec((None, mod_rows, D_MODEL), lambda i, j: (i, 0, 0))


def _ffn_kernel(x_ref, g_ref, sh_ref, sc_ref, gt_ref, wg_ref, wu_ref, wd_ref, o_ref, h_ref, *, n_chunk):
    j = pl.program_id(1)

    @pl.when(j == 0)
    def _():
        _norm_modulate_into(h_ref, x_ref, g_ref, sh_ref, sc_ref)
        o_ref[...] = jnp.zeros_like(o_ref)

    h = h_ref[...]
    a = (_silu(_dot(h, wg_ref[...])) * _dot(h, wu_ref[...])).astype(BF16)
    for n0 in range(0, D_MODEL, n_chunk):
        o_ref[:, n0:n0 + n_chunk] += _dot(a, wd_ref[:, n0:n0 + n_chunk])

    @pl.when(j == pl.num_programs(1) - 1)
    def _():
        rows = x_ref.shape[0]
        per_row = gt_ref.shape[0] != 1
        step = min(rows, 128)
        for r0 in range(0, rows, step):
            gt = gt_ref[r0:r0 + step, :] if per_row else gt_ref[...]
            o_ref[r0:r0 + step, :] = x_ref[r0:r0 + step, :] + (0.5 * gt) * o_ref[r0:r0 + step, :]


def _ffn(x, g, mods, wg, wu, wd, bm):
    rows = x.shape[0]
    d_ff = wg.shape[1]
    tf = 256
    shift, scale, gate = mods
    mr = shift.shape[1]
    return pl.pallas_call(
        functools.partial(_ffn_kernel, n_chunk=512),
        grid=(rows // bm, d_ff // tf),
        in_specs=[pl.BlockSpec((bm, D_MODEL), lambda i, j: (i, 0)),
                  pl.BlockSpec((1, D_MODEL), lambda i, j: (0, 0)),
                  _mod_spec(mr, bm), _mod_spec(mr, bm), _mod_spec(mr, bm),
                  pl.BlockSpec((D_MODEL, tf), lambda i, j: (0, j)),
                  pl.BlockSpec((D_MODEL, tf), lambda i, j: (0, j)),
                  pl.BlockSpec((tf, D_MODEL), lambda i, j: (j, 0))],
        out_specs=pl.BlockSpec((bm, D_MODEL), lambda i, j: (i, 0)),
        out_shape=jax.ShapeDtypeStruct((rows, D_MODEL), F32),
        scratch_shapes=[pltpu.VMEM((bm, D_MODEL), BF16)],
        compiler_params=_cparams("parallel", "arbitrary"),
    )(x, g.reshape(1, D_MODEL), shift, scale, gate, wg, wu, wd)


def _proj_kernel(x_ref, g_ref, sh_ref, sc_ref, w_ref, o_ref, h_ref):
    @pl.when(pl.program_id(1) == 0)
    def _():
        _norm_modulate_into(h_ref, x_ref, g_ref, sh_ref, sc_ref)

    o_ref[...] = _dot(h_ref[...], w_ref[...])


def _proj(x, g, mods, w, bm):
    rows = x.shape[0]
    n = w.shape[1]
    bn = 512
    shift, scale, _ = mods
    mr = shift.shape[1]
    return pl.pallas_call(
        _proj_kernel,
        grid=(rows // bm, n // bn),
        in_specs=[pl.BlockSpec((bm, D_MODEL), lambda i, j: (i, 0)),
                  pl.BlockSpec((1, D_MODEL), lambda i, j: (0, 0)),
                  _mod_spec(mr, bm), _mod_spec(mr, bm),
                  pl.BlockSpec((D_MODEL, bn), lambda i, j: (0, j))],
        out_specs=pl.BlockSpec((bm, bn), lambda i, j: (i, j)),
        out_shape=jax.ShapeDtypeStruct((rows, n), F32),
        scratch_shapes=[pltpu.VMEM((bm, D_MODEL), BF16)],
        compiler_params=_cparams("parallel", "arbitrary"),
    )(x, g.reshape(1, D_MODEL), shift, scale, w)


def _outproj_kernel(x_ref, yr_ref, yn_ref, gt_ref, wa_ref, wb_ref, o_ref):
    y = _dot(yr_ref[...].astype(BF16), wa_ref[...]) + _dot(yn_ref[...].astype(BF16), wb_ref[...])
    per_row = gt_ref.shape[0] != 1
    o_ref[...] = x_ref[...] + (gt_ref[...] if per_row else gt_ref[...]) * y


def _outproj(x, y_ret, y_nsa, gate, w_out, bm):
    rows = x.shape[0]
    half = y_ret.shape[1]
    bn = 512
    mr = gate.shape[1]
    return pl.pallas_call(
        _outproj_kernel,
        grid=(rows // bm, D_MODEL // bn),
        in_specs=[pl.BlockSpec((bm, bn), lambda i, j: (i, j)),
                  pl.BlockSpec((bm, half), lambda i, j: (i, 0)),
                  pl.BlockSpec((bm, half), lambda i, j: (i, 0)),
                  pl.BlockSpec((None, mr, bn), lambda i, j: (i, 0, j)),
                  pl.BlockSpec((half, bn), lambda i, j: (0, j)),
                  pl.BlockSpec((half, bn), lambda i, j: (1, j))],
        out_specs=pl.BlockSpec((bm, bn), lambda i, j: (i, j)),
        out_shape=jax.ShapeDtypeStruct((rows, D_MODEL), F32),
        compiler_params=_cparams("parallel", "arbitrary"),
    )(x, y_ret, y_nsa, gate, w_out, w_out)


def _final_norm_kernel(x_ref, g_ref, o_ref):
    x = x_ref[...]
    o_ref[...] = x * lax.rsqrt(jnp.mean(x * x, axis=-1, keepdims=True) + RMS_EPS) * g_ref[...]


def _final_norm(x, g, bm):
    rows = x.shape[0]
    return pl.pallas_call(
        _final_norm_kernel,
        grid=(rows // bm,),
        in_specs=[pl.BlockSpec((bm, D_MODEL), lambda i: (i, 0)),
                  pl.BlockSpec((1, D_MODEL), lambda i: (0, 0))],
        out_specs=pl.BlockSpec((bm, D_MODEL), lambda i: (i, 0)),
        out_shape=jax.ShapeDtypeStruct((rows, D_MODEL), F32),
        compiler_params=_cparams("parallel"),
    )(x, g.reshape(1, D_MODEL))


def _retention_kernel(q_ref, k_ref, v_ref, rg_ref, cos_ref, sin_ref, dm_ref, qd_ref, kd_ref, cd_ref, s0_ref,
                      y_ref, s_ref, *pad, cq):
    @pl.when(pl.program_id(1) == 0)
    def _():
        s_ref[...] = s0_ref[...]

    cos = cos_ref[...]
    sin = sin_ref[...]
    padded = cq != RET_CHUNK
    if padded:
        kp_ref, vp_ref = pad
        kp_ref[...] = jnp.zeros_like(kp_ref)
        vp_ref[...] = jnp.zeros_like(vp_ref)
    for h in range(RET_HEADS):
        q = q_ref[:, h * RET_DK:(h + 1) * RET_DK]
        k = k_ref[:, h * RET_DK:(h + 1) * RET_DK]
        q = q * cos + pltpu.roll(q, RET_DK // 2, 1) * sin
        k = (k * cos + pltpu.roll(k, RET_DK // 2, 1) * sin) * (RET_DK ** -0.5)
        v = v_ref[:, h * RET_DV:(h + 1) * RET_DV]
        if padded:
            kp_ref[0:cq, :] = k
            vp_ref[0:cq, :] = v
            k = kp_ref[...]
            v = vp_ref[...]
        qb = q.astype(BF16)
        vb = v.astype(BF16)
        att = _dot_nt(qb, k.astype(BF16)) * dm_ref[h]
        s = s_ref[h]
        o = _dot(att.astype(BF16), vb) + _dot(qb, s.astype(BF16)) * qd_ref[h]
        kd = (k * kd_ref[h]).T.astype(BF16)
        s_ref[h] = s * cd_ref[h] + _dot(kd, vb)
        o = o * lax.rsqrt(jnp.mean(o * o, axis=-1, keepdims=True) + RMS_EPS)
        y = _silu(rg_ref[:, h * RET_DV:(h + 1) * RET_DV]) * o
        y_ref[:, h * RET_DV:(h + 1) * RET_DV] = y.astype(y_ref.dtype)


def _retention_tables(pos, chunk, cq):
    half = RET_DK // 2
    inv = jnp.power(ROPE_BASE, -jnp.arange(half, dtype=F32) * 2.0 / RET_DK)
    ang = pos.astype(F32)[:, None] * inv[None, :]
    cos = jnp.cos(ang)
    sin = jnp.sin(ang)
    cos2 = jnp.concatenate([cos, cos], axis=-1)
    sin2 = jnp.concatenate([-sin, sin], axis=-1)
    log_g = jnp.log(1.0 - jnp.exp2(-5.0 - jnp.arange(RET_HEADS, dtype=F32)))
    i = jnp.arange(chunk, dtype=F32)
    diff = i[:, None] - i[None, :]
    dmask = jnp.where(diff >= 0, jnp.exp(log_g[:, None, None] * jnp.maximum(diff, 0.0)), 0.0)
    q_dec = jnp.exp(log_g[None, :] * (i[:, None] + 1.0))
    k_dec = jnp.exp(log_g[None, :] * (chunk - 1.0 - i[:, None]))
    c_dec = jnp.exp(log_g * chunk)
    dm = jnp.zeros((RET_HEADS, cq, RET_CHUNK), F32).at[:, :chunk, :chunk].set(dmask)
    qd = jnp.zeros((RET_HEADS, cq, RET_DV), F32).at[:, :chunk, :].set(
        jnp.broadcast_to(q_dec.T[:, :, None], (RET_HEADS, chunk, RET_DV)))
    kd = jnp.zeros((RET_HEADS, RET_CHUNK, RET_DK), F32).at[:, :chunk, :].set(
        jnp.broadcast_to(k_dec.T[:, :, None], (RET_HEADS, chunk, RET_DK)))
    cd = jnp.broadcast_to(c_dec[:, None, None], (RET_HEADS, 1, RET_DV))
    return cos2, sin2, dm, qd, kd, cd


def _retention(proj, s0, pos, batch, t_rows, chunk, cq, out_dtype):
    nc = t_rows // cq
    cos2, sin2, dm, qd, kd, cd = _retention_tables(pos, chunk, cq)
    hk = RET_HEADS * RET_DK
    hv = RET_HEADS * RET_DV
    row = lambda b, c: b * nc + c
    scratch = [] if cq == RET_CHUNK else [pltpu.VMEM((RET_CHUNK, RET_DK), F32), pltpu.VMEM((RET_CHUNK, RET_DV), F32)]
    full3 = lambda shape: pl.BlockSpec(shape, lambda b, c: (0, 0, 0))
    return pl.pallas_call(
        functools.partial(_retention_kernel, cq=cq),
        grid=(batch, nc),
        in_specs=[pl.BlockSpec((cq, hk), lambda b, c: (row(b, c), COL_RQ // hk)),
                  pl.BlockSpec((cq, hk), lambda b, c: (row(b, c), COL_RK // hk)),
                  pl.BlockSpec((cq, hv), lambda b, c: (row(b, c), COL_RV // hv)),
                  pl.BlockSpec((cq, hv), lambda b, c: (row(b, c), COL_RG // hv)),
                  pl.BlockSpec((cq, RET_DK), lambda b, c: (c, 0)),
                  pl.BlockSpec((cq, RET_DK), lambda b, c: (c, 0)),
                  full3(dm.shape), full3(qd.shape), full3(kd.shape), full3(cd.shape),
                  pl.BlockSpec((None, RET_HEADS, RET_DK, RET_DV), lambda b, c: (b, 0, 0, 0))],
        out_specs=[pl.BlockSpec((cq, hv), lambda b, c: (row(b, c), 0)),
                   pl.BlockSpec((None, RET_HEADS, RET_DK, RET_DV), lambda b, c: (b, 0, 0, 0))],
        out_shape=[jax.ShapeDtypeStruct((batch * t_rows, hv), out_dtype),
                   jax.ShapeDtypeStruct((batch, RET_HEADS, RET_DK, RET_DV), F32)],
        scratch_shapes=scratch,
        compiler_params=_cparams("parallel", "arbitrary"),
    )(proj, proj, proj, proj, cos2, sin2, dm, qd, kd, cd, s0)


def _compress_kernel(*refs, n_src):
    srcs = refs[:n_src]
    perm_ref, w1_ref, pe_ref, b1_ref, w2_ref, o_ref, x_ref, prev_ref = refs[n_src:]
    grp = perm_ref.shape[0]
    segs = grp // CMP_STRIDE
    n_grp = srcs[0].shape[0] // grp
    s_tot = n_src * n_grp * segs
    rows = NSA_KV_HEADS * s_tot

    @pl.when(pl.program_id(1) == 0)
    def _():
        prev_ref[...] = jnp.zeros_like(prev_ref)

    for si, src in enumerate(srcs):
        for gi in range(n_grp):
            blk = _dot(perm_ref[...], src[gi * grp:(gi + 1) * grp, :].astype(BF16))
            r0 = (si * n_grp + gi) * segs
            for s in range(CMP_STRIDE):
                for c in range(2):
                    for g in range(NSA_KV_HEADS):
                        col = (c * NSA_KV_HEADS + g) * NSA_DH
                        x_ref[c, g * s_tot + r0:g * s_tot + r0 + segs, s * NSA_DH:(s + 1) * NSA_DH] = (
                            blk[s * segs:(s + 1) * segs, col:col + NSA_DH])

    first = (lax.broadcasted_iota(jnp.int32, (rows, CMP_HIDDEN), 0) & (s_tot - 1)) == 0
    for c in range(2):
        w1 = w1_ref[c]
        part = _dot(x_ref[c].astype(BF16), w1)
        pe_part = _dot(pe_ref[c], w1)
        bias = b1_ref[c] + pe_part[0:1, :CMP_HIDDEN] + pe_part[1:2, CMP_HIDDEN:]
        p0 = part[:, :CMP_HIDDEN]
        p1 = part[:, CMP_HIDDEN:]
        shifted = jnp.where(first, pltpu.roll(prev_ref[c], rows - s_tot + 1, 0), pltpu.roll(p0, 1, 0))
        prev_ref[c] = p0
        kc = _dot(_silu(bias + shifted + p1).astype(BF16), w2_ref[c])
        for g in range(NSA_KV_HEADS):
            col = (c * NSA_KV_HEADS + g) * NSA_DH
            o_ref[:, col:col + NSA_DH] = kc[g * s_tot:(g + 1) * s_tot].astype(o_ref.dtype)


def _compress_weights(cmp_pe, cmp_w1, cmp_b1, cmp_w2):
    w1 = cmp_w1.reshape(2, CMP_RATIO, CMP_STRIDE, NSA_DH, CMP_HIDDEN).transpose(0, 2, 3, 1, 4)
    w1 = w1.reshape(2, CMP_STRIDE * NSA_DH, CMP_RATIO * CMP_HIDDEN).astype(BF16)
    pe = cmp_pe.reshape(2, CMP_RATIO, CMP_STRIDE * NSA_DH)
    pe = jnp.pad(pe, [(0, 0), (0, 8 - CMP_RATIO), (0, 0)]).astype(BF16)
    return _segment_permutation(), w1, pe, cmp_b1.reshape(2, 1, CMP_HIDDEN), cmp_w2.astype(BF16)


COMPRESS_GROUP_ROWS = 128


def _segment_permutation():
    segs = COMPRESS_GROUP_ROWS // CMP_STRIDE
    r = np.arange(COMPRESS_GROUP_ROWS)
    src = (r % segs) * CMP_STRIDE + r // segs
    return jnp.asarray(src[:, None] == np.arange(COMPRESS_GROUP_ROWS)[None, :], BF16)


def _compress_specs(idx):
    kdim = CMP_STRIDE * NSA_DH
    return [pl.BlockSpec((COMPRESS_GROUP_ROWS, COMPRESS_GROUP_ROWS), idx(lambda *_: (0, 0))),
            pl.BlockSpec((2, kdim, CMP_RATIO * CMP_HIDDEN), idx(lambda *_: (0, 0, 0))),
            pl.BlockSpec((2, 8, kdim), idx(lambda *_: (0, 0, 0))),
            pl.BlockSpec((2, 1, CMP_HIDDEN), idx(lambda *_: (0, 0, 0))),
            pl.BlockSpec((2, CMP_HIDDEN, NSA_DH), idx(lambda *_: (0, 0, 0)))]


def _compress_scratch(s_tot):
    rows = NSA_KV_HEADS * s_tot
    return [pltpu.VMEM((2, rows, CMP_STRIDE * NSA_DH), F32), pltpu.VMEM((2, rows, CMP_HIDDEN), F32)]


def _compress_prompt(proj, cmp_w, batch, t):
    segs = t // CMP_STRIDE
    return pl.pallas_call(
        functools.partial(_compress_kernel, n_src=1),
        grid=(batch, 1),
        in_specs=[pl.BlockSpec((t, 2 * KV_W), lambda b, i: (b, COL_KVC // (2 * KV_W)))]
        + _compress_specs(lambda f: f),
        out_specs=pl.BlockSpec((None, segs, 2 * KV_W), lambda b, i: (b, 0, 0)),
        out_shape=jax.ShapeDtypeStruct((batch, segs, 2 * KV_W), BF16),
        scratch_shapes=_compress_scratch(segs),
        compiler_params=_cparams("parallel", "arbitrary"),
    )(proj, *cmp_w)


def _split_bf16(x):
    hi = x.astype(BF16)
    return hi, (x - hi.astype(F32)).astype(BF16)


def _selection_scores(psl_t, t_pos):
    j = lax.broadcasted_iota(jnp.int32, psl_t.shape, 0)
    jc = t_pos // SEL_BLOCK
    forced = (j == 0) | (j == jc) | (j == jc - 1)
    return jnp.where(j > jc, NEG_INF, jnp.where(forced, FORCE_SCORE, psl_t))


def _beats(row, jp, score, j):
    return jnp.where(row > score, 1.0, jnp.where(row == score, jnp.where(j > jp, 1.0, 0.0), 0.0))


def _nsa_prompt_kernel(q_ref, gt_ref, kck_ref, kcv_ref, ks_ref, vs_ref, kw_ref, vw_ref, at_ref, e_ref, o_ref,
                       ksb, vsb, kwb, vwb, m_ref, l_ref, acc_ref, *, tq, tk, t_len, n_sel):
    qi = pl.program_id(2)
    rows = NSA_HPG * tq

    @pl.when(qi == 0)
    def _():
        ksb[...] = ks_ref[...].astype(BF16)
        vsb[...] = vs_ref[...].astype(BF16)
        kwb[...] = kw_ref[...].astype(BF16)
        vwb[...] = vw_ref[...].astype(BF16)

    def t_of(shape):
        return qi * tq + (lax.broadcasted_iota(jnp.int32, shape, 0) & (tq - 1))

    q4 = jnp.concatenate([q_ref[:, p * NSA_DH:(p + 1) * NSA_DH] for p in range(NSA_HPG)], axis=0).astype(BF16)

    n_slot = kck_ref.shape[0]
    slot = lax.broadcasted_iota(jnp.int32, (rows, n_slot), 1)
    valid = (slot >= 1) & (slot * CMP_STRIDE + (CMP_BLOCK - CMP_STRIDE - 1) <= t_of((rows, n_slot)))
    p_c = _masked_softmax(_dot_nt(q4, kck_ref[...]) * ATT_SCALE, valid)
    o_c = _dot(p_c.astype(BF16), kcv_ref[...])

    pg = p_c[0:tq]
    for p in range(1, NSA_HPG):
        pg = pg + p_c[p * tq:(p + 1) * tq]
    hi, lo = _split_bf16(pg)
    psl_t = _dot_nt(at_ref[...], hi) + _dot_nt(at_ref[...], lo)
    jshape = psl_t.shape
    j = lax.broadcasted_iota(jnp.int32, jshape, 0)
    score = _selection_scores(psl_t, qi * tq + lax.broadcasted_iota(jnp.int32, jshape, 1))
    rank = jnp.zeros(jshape, F32)
    for jp in range(n_sel):
        rank = rank + _beats(score[jp:jp + 1, :], jp, score, j)
    sel_t = jnp.where(rank < min(SEL_TOPK, n_sel), 1.0, 0.0).astype(BF16)
    eye = jnp.where(lax.broadcasted_iota(jnp.int32, (tq, tq), 0) == lax.broadcasted_iota(jnp.int32, (tq, tq), 1),
                    1.0, 0.0).astype(BF16)
    sel = _dot_nt(eye, sel_t).astype(BF16)

    m_ref[...] = jnp.full(m_ref.shape, NEG_INF, F32)
    l_ref[...] = jnp.zeros(l_ref.shape, F32)
    acc_ref[...] = jnp.zeros(acc_ref.shape, F32)

    def chunk(c, carry):
        k0 = pl.multiple_of(c * tk, tk)
        s = _dot_nt(q4, ksb[pl.ds(k0, tk), :]) * ATT_SCALE
        picked = _dot(sel, e_ref[c])
        picked = jnp.concatenate([picked] * NSA_HPG, axis=0)
        kpos = k0 + lax.broadcasted_iota(jnp.int32, (rows, tk), 1)
        ok = jnp.where(kpos <= t_of((rows, tk)), picked, 0.0) > 0.5
        s = jnp.where(ok, s, NEG_INF)
        m_old = m_ref[...]
        m_new = jnp.maximum(m_old, jnp.max(s, axis=-1, keepdims=True))
        alpha = jnp.exp(m_old - m_new)
        e = jnp.where(ok, jnp.exp(s - m_new), 0.0)
        l_ref[...] = l_ref[...] * alpha + jnp.sum(e, axis=-1, keepdims=True)
        acc_ref[...] = acc_ref[...] * alpha + _dot(e.astype(BF16), vsb[pl.ds(k0, tk), :])
        m_ref[...] = m_new
        return carry

    lax.fori_loop(0, (qi * tq) // tk + 1, chunk, 0)
    o_s = acc_ref[...] / jnp.maximum(l_ref[...], 1e-30)

    n_back = WINDOW // tq
    w_len = min((n_back + 1) * tq, t_len)
    w0 = pl.multiple_of(jnp.maximum(qi - n_back, 0) * tq, tq)
    kpos = w0 + lax.broadcasted_iota(jnp.int32, (rows, w_len), 1)
    tw = t_of((rows, w_len))
    okw = jnp.where(kpos <= tw, jnp.where(kpos > tw - WINDOW, 1.0, 0.0), 0.0) > 0.5
    p_w = _masked_softmax(_dot_nt(q4, kwb[pl.ds(w0, w_len), :]) * ATT_SCALE, okw)
    o_w = _dot(p_w.astype(BF16), vwb[pl.ds(w0, w_len), :])

    gt = jax.nn.sigmoid(gt_ref[...])
    for p in range(NSA_HPG):
        r = slice(p * tq, (p + 1) * tq)
        y = (gt[:, p:p + 1] * o_c[r] + gt[:, NSA_HPG + p:NSA_HPG + p + 1] * o_s[r]
             + gt[:, 2 * NSA_HPG + p:2 * NSA_HPG + p + 1] * o_w[r])
        o_ref[:, p * NSA_DH:(p + 1) * NSA_DH] = y.astype(o_ref.dtype)


def _selection_sum_matrix(n_rows, n_slot, n_sel):
    ratio = SEL_BLOCK // CMP_STRIDE
    j = np.arange(n_rows)[:, None]
    m = np.arange(n_slot)[None, :]
    a = (j < n_sel) & (m >= ratio * j) & (m <= ratio * j + ratio + CMP_RATIO - 2)
    return jnp.asarray(a, BF16)


def _block_expand_matrix(n_chunks, n_rows, tk, blocks_per_chunk):
    c = np.arange(n_chunks)[:, None, None]
    j = np.arange(n_rows)[None, :, None]
    k = np.arange(tk)[None, None, :]
    return jnp.asarray((c * blocks_per_chunk + k // SEL_BLOCK) == j, BF16)


def _nsa_prompt(proj, kc, batch, t):
    tq, tk = 128, 512
    nq = t // tq
    n_sel = t // SEL_BLOCK
    n_slot = kc.shape[1]
    jrows = 128
    a_t = _selection_sum_matrix(jrows, n_slot, n_sel)
    e = _block_expand_matrix(t // tk, jrows, tk, tk // SEL_BLOCK)
    qw = NSA_HPG * NSA_DH
    kv = lambda base, c: pl.BlockSpec((t, NSA_DH), lambda b, g, qi: (b, (base + c * KV_W) // NSA_DH + g))
    return pl.pallas_call(
        functools.partial(_nsa_prompt_kernel, tq=tq, tk=tk, t_len=t, n_sel=n_sel),
        grid=(batch, NSA_KV_HEADS, nq),
        in_specs=[pl.BlockSpec((tq, qw), lambda b, g, qi: (b * nq + qi, COL_NQ // qw + g)),
                  pl.BlockSpec((tq, LANE), lambda b, g, qi: (b * nq + qi, COL_NG // LANE + g)),
                  pl.BlockSpec((None, n_slot, NSA_DH), lambda b, g, qi: (b, 0, g)),
                  pl.BlockSpec((None, n_slot, NSA_DH), lambda b, g, qi: (b, 0, NSA_KV_HEADS + g)),
                  kv(COL_KVS, 0), kv(COL_KVS, 1), kv(COL_KVW, 0), kv(COL_KVW, 1),
                  pl.BlockSpec(a_t.shape, lambda b, g, qi: (0, 0)),
                  pl.BlockSpec(e.shape, lambda b, g, qi: (0, 0, 0))],
        out_specs=pl.BlockSpec((tq, qw), lambda b, g, qi: (b * nq + qi, g)),
        out_shape=jax.ShapeDtypeStruct((batch * t, NSA_HEADS * NSA_DH), BF16),
        scratch_shapes=[pltpu.VMEM((t, NSA_DH), BF16)] * 4
        + [pltpu.VMEM((NSA_HPG * tq, 1), F32), pltpu.VMEM((NSA_HPG * tq, 1), F32),
           pltpu.VMEM((NSA_HPG * tq, NSA_DH), F32)],
        compiler_params=_cparams("parallel", "parallel", "arbitrary"),
    )(proj, proj, kc, kc, proj, proj, proj, proj, a_t, e)


PAGES_PER_STEP = 8


def _page_specs(layer, page_rows, width):
    def spec(k):
        return pl.BlockSpec((None, None, page_rows, width),
                            lambda b, i, pt: (layer, pt[b, i * PAGES_PER_STEP + k], 0, 0))
    return [spec(k) for k in range(PAGES_PER_STEP)]


def _compress_sample_kernel(pt_ref, *refs):
    _compress_kernel(*refs, n_src=PAGES_PER_STEP)


def _compress_sample(pool, layer, page_table, cmp_w):
    batch, n_pages = page_table.shape
    page_rows = pool.shape[2]
    segs = page_rows // CMP_STRIDE
    s_tot = PAGES_PER_STEP * segs
    steps = n_pages // PAGES_PER_STEP
    grid_spec = pltpu.PrefetchScalarGridSpec(
        num_scalar_prefetch=1, grid=(batch, steps),
        in_specs=_page_specs(layer, page_rows, 2 * KV_W) + _compress_specs(lambda f: f),
        out_specs=pl.BlockSpec((None, s_tot, 2 * KV_W), lambda b, i, pt: (b, i, 0)),
        scratch_shapes=_compress_scratch(s_tot))
    return pl.pallas_call(
        _compress_sample_kernel,
        grid_spec=grid_spec,
        out_shape=jax.ShapeDtypeStruct((batch, steps * s_tot, 2 * KV_W), BF16),
        compiler_params=_cparams("parallel", "arbitrary"),
    )(page_table, *([pool] * PAGES_PER_STEP), *cmp_w)


def _stack_heads(q_ref, g):
    return jnp.concatenate([q_ref[:, (g * NSA_HPG + p) * NSA_DH:(g * NSA_HPG + p + 1) * NSA_DH]
                            for p in range(NSA_HPG)], axis=0).astype(BF16)


def _nsa_sample_select_kernel(q_ref, kck_ref, kcv_ref, at_ref, oc_ref, selt_ref, pg_ref, sc_ref, rank_ref, *,
                              past_len, n_sel):
    rq = SAMPLE_ROWS
    rows = NSA_HPG * rq
    n_slot = kck_ref.shape[0]
    pg_ref[...] = jnp.zeros_like(pg_ref)
    slot = lax.broadcasted_iota(jnp.int32, (rows, n_slot), 1)
    t_q = past_len + (lax.broadcasted_iota(jnp.int32, (rows, n_slot), 0) & (rq - 1))
    valid = (slot >= 1) & (slot * CMP_STRIDE + (CMP_BLOCK - CMP_STRIDE - 1) <= t_q)
    for g in range(NSA_KV_HEADS):
        q4 = _stack_heads(q_ref, g)
        p_c = _masked_softmax(_dot_nt(q4, kck_ref[:, g * NSA_DH:(g + 1) * NSA_DH]) * ATT_SCALE, valid)
        o_c = _dot(p_c.astype(BF16), kcv_ref[:, g * NSA_DH:(g + 1) * NSA_DH])
        pg = p_c[0:rq]
        for p in range(NSA_HPG):
            col = (g * NSA_HPG + p) * NSA_DH
            oc_ref[:, col:col + NSA_DH] = o_c[p * rq:(p + 1) * rq]
            if p:
                pg = pg + p_c[p * rq:(p + 1) * rq]
        pg_ref[g * rq:(g + 1) * rq, :] = pg
    hi, lo = _split_bf16(pg_ref[...])
    psl_t = _dot_nt(at_ref[...], hi) + _dot_nt(at_ref[...], lo)
    jshape = psl_t.shape
    t_pos = past_len + (lax.broadcasted_iota(jnp.int32, jshape, 1) & (rq - 1))
    sc_ref[...] = _selection_scores(psl_t, t_pos)
    rank_ref[...] = jnp.zeros_like(rank_ref)

    def body(jp, carry):
        j = lax.broadcasted_iota(jnp.int32, jshape, 0)
        rank_ref[...] += _beats(sc_ref[pl.ds(jp, 1), :], jp, sc_ref[...], j)
        return carry

    lax.fori_loop(0, n_sel, body, 0)
    selt_ref[...] = jnp.where(rank_ref[...] < min(SEL_TOPK, n_sel), 1.0, 0.0).astype(selt_ref.dtype)


SEL_ROWS_SAMPLE = 384


def _nsa_sample_select(proj, kc, batch, past_len, n_sel):
    n_slot = kc.shape[1]
    a_t = _selection_sum_matrix(SEL_ROWS_SAMPLE, n_slot, n_sel)
    qw = NSA_HEADS * NSA_DH
    return pl.pallas_call(
        functools.partial(_nsa_sample_select_kernel, past_len=past_len, n_sel=n_sel),
        grid=(batch,),
        in_specs=[pl.BlockSpec((SAMPLE_ROWS, qw), lambda b: (b, COL_NQ // qw)),
                  pl.BlockSpec((None, n_slot, KV_W), lambda b: (b, 0, 0)),
                  pl.BlockSpec((None, n_slot, KV_W), lambda b: (b, 0, 1)),
                  pl.BlockSpec(a_t.shape, lambda b: (0, 0))],
        out_specs=[pl.BlockSpec((SAMPLE_ROWS, qw), lambda b: (b, 0)),
                   pl.BlockSpec((None, SEL_ROWS_SAMPLE, LANE), lambda b: (b, 0, 0))],
        out_shape=[jax.ShapeDtypeStruct((batch * SAMPLE_ROWS, qw), F32),
                   jax.ShapeDtypeStruct((batch, SEL_ROWS_SAMPLE, LANE), BF16)],
        scratch_shapes=[pltpu.VMEM((LANE, n_slot), F32), pltpu.VMEM((SEL_ROWS_SAMPLE, LANE), F32),
                        pltpu.VMEM((SEL_ROWS_SAMPLE, LANE), F32)],
        compiler_params=_cparams("parallel"),
    )(proj, kc, kc, a_t)


def _nsa_sample_attend_kernel(pt_ref, *refs, past_len, page_rows):
    pages = refs[:PAGES_PER_STEP]
    (q_ref, selt_ref, e_ref, ksn_ref, kwn_ref, win_ref, gt_ref, oc_ref, o_ref,
     m_ref, l_ref, acc_ref, pad_ref) = refs[PAGES_PER_STEP:]
    i = pl.program_id(1)
    rq = SAMPLE_ROWS
    rows = NSA_HPG * rq
    blocks_per_step = PAGES_PER_STEP * page_rows // SEL_BLOCK

    @pl.when(i == 0)
    def _():
        m_ref[...] = jnp.full(m_ref.shape, NEG_INF, F32)
        l_ref[...] = jnp.zeros(l_ref.shape, F32)
        acc_ref[...] = jnp.zeros(acc_ref.shape, F32)

    def t_of(shape):
        return past_len + (lax.broadcasted_iota(jnp.int32, shape, 0) & (rq - 1))

    def online_update(g, s, ok, v):
        s = jnp.where(ok, s, NEG_INF)
        m_old = m_ref[g]
        m_new = jnp.maximum(m_old, jnp.max(s, axis=-1, keepdims=True))
        alpha = jnp.exp(m_old - m_new)
        e = jnp.where(ok, jnp.exp(s - m_new), 0.0)
        l_ref[g] = l_ref[g] * alpha + jnp.sum(e, axis=-1, keepdims=True)
        acc_ref[g] = acc_ref[g] * alpha + _dot(e.astype(BF16), v)
        m_ref[g] = m_new

    eye = jnp.where(lax.broadcasted_iota(jnp.int32, (LANE, LANE), 0) == lax.broadcasted_iota(jnp.int32, (LANE, LANE), 1),
                    1.0, 0.0).astype(BF16)

    j0 = pl.multiple_of(i * blocks_per_step, blocks_per_step)
    sel = _dot_nt(eye, selt_ref[pl.ds(j0, LANE), :]).astype(BF16)
    picked = _dot(sel, e_ref[...])
    for g in range(NSA_KV_HEADS):
        kg = jnp.concatenate([pg[:, g * NSA_DH:(g + 1) * NSA_DH] for pg in pages], axis=0).astype(BF16)
        vg = jnp.concatenate([pg[:, KV_W + g * NSA_DH:KV_W + (g + 1) * NSA_DH] for pg in pages], axis=0).astype(BF16)
        ok = jnp.concatenate([picked[g * rq:(g + 1) * rq]] * NSA_HPG, axis=0) > 0.5
        online_update(g, _dot_nt(_stack_heads(q_ref, g), kg) * ATT_SCALE, ok, vg)

    @pl.when(i == pl.num_programs(1) - 1)
    def _():
        n_past_blocks = past_len // SEL_BLOCK
        pad_ref[...] = jnp.zeros_like(pad_ref)
        pad_ref[0:rq, :] = ksn_ref[...]
        sel_new = _dot_nt(eye, selt_ref[n_past_blocks:n_past_blocks + LANE, :])
        kpos = past_len + lax.broadcasted_iota(jnp.int32, (rows, LANE), 1)
        causal = kpos <= t_of((rows, LANE))
        o_s = []
        for g in range(NSA_KV_HEADS):
            kn = pad_ref[:, g * NSA_DH:(g + 1) * NSA_DH].astype(BF16)
            vn = pad_ref[:, KV_W + g * NSA_DH:KV_W + (g + 1) * NSA_DH].astype(BF16)
            pk = jnp.concatenate([sel_new[g * rq:(g + 1) * rq, 0:1]] * NSA_HPG, axis=0)
            ok = jnp.where(causal, pk, 0.0) > 0.5
            online_update(g, _dot_nt(_stack_heads(q_ref, g), kn) * ATT_SCALE, ok, vn)
            o_s.append(acc_ref[g] / jnp.maximum(l_ref[g], 1e-30))

        pad_ref[0:rq, :] = kwn_ref[...]
        n_buf = win_ref.shape[0]
        kposw = past_len - n_buf + lax.broadcasted_iota(jnp.int32, (rows, n_buf + LANE), 1)
        tw = t_of((rows, n_buf + LANE))
        okw = jnp.where(kposw <= tw, jnp.where(kposw > tw - WINDOW, 1.0, 0.0), 0.0) > 0.5
        gt = jax.nn.sigmoid(gt_ref[...])
        for g in range(NSA_KV_HEADS):
            kw = jnp.concatenate([win_ref[:, g * NSA_DH:(g + 1) * NSA_DH],
                                  pad_ref[:, g * NSA_DH:(g + 1) * NSA_DH]], axis=0).astype(BF16)
            vw = jnp.concatenate([win_ref[:, KV_W + g * NSA_DH:KV_W + (g + 1) * NSA_DH],
                                  pad_ref[:, KV_W + g * NSA_DH:KV_W + (g + 1) * NSA_DH]], axis=0).astype(BF16)
            p_w = _masked_softmax(_dot_nt(_stack_heads(q_ref, g), kw) * ATT_SCALE, okw)
            o_w = _dot(p_w.astype(BF16), vw)
            for p in range(NSA_HPG):
                col = (g * NSA_HPG + p) * NSA_DH
                r = slice(p * rq, (p + 1) * rq)
                gc = g * LANE + p
                y = (gt[:, gc:gc + 1] * oc_ref[:, col:col + NSA_DH]
                     + gt[:, gc + NSA_HPG:gc + NSA_HPG + 1] * o_s[g][r]
                     + gt[:, gc + 2 * NSA_HPG:gc + 2 * NSA_HPG + 1] * o_w[r])
                o_ref[:, col:col + NSA_DH] = y


def _nsa_sample_attend(proj, o_c, sel_t, sel_pool, win_kv, layer, page_table, past_len):
    batch, n_pages = page_table.shape
    page_rows = sel_pool.shape[2]
    steps = n_pages // PAGES_PER_STEP
    keys = PAGES_PER_STEP * page_rows
    e = _block_expand_matrix(1, LANE, keys, keys // SEL_BLOCK)[0]
    n_buf = win_kv.shape[2]
    qw = NSA_HEADS * NSA_DH
    rq = SAMPLE_ROWS
    grid_spec = pltpu.PrefetchScalarGridSpec(
        num_scalar_prefetch=1, grid=(batch, steps),
        in_specs=_page_specs(layer, page_rows, 2 * KV_W) + [
            pl.BlockSpec((rq, qw), lambda b, i, pt: (b, COL_NQ // qw)),
            pl.BlockSpec((None, SEL_ROWS_SAMPLE, LANE), lambda b, i, pt: (b, 0, 0)),
            pl.BlockSpec(e.shape, lambda b, i, pt: (0, 0)),
            pl.BlockSpec((rq, 2 * KV_W), lambda b, i, pt: (b, COL_KVS // (2 * KV_W))),
            pl.BlockSpec((rq, 2 * KV_W), lambda b, i, pt: (b, COL_KVW // (2 * KV_W))),
            pl.BlockSpec((None, None, n_buf, 2 * KV_W), lambda b, i, pt: (layer, b, 0, 0)),
            pl.BlockSpec((rq, NSA_KV_HEADS * LANE), lambda b, i, pt: (b, COL_NG // (NSA_KV_HEADS * LANE))),
            pl.BlockSpec((rq, qw), lambda b, i, pt: (b, 0))],
        out_specs=pl.BlockSpec((rq, qw), lambda b, i, pt: (b, 0)),
        scratch_shapes=[pltpu.VMEM((NSA_KV_HEADS, NSA_HPG * rq, 1), F32),
                        pltpu.VMEM((NSA_KV_HEADS, NSA_HPG * rq, 1), F32),
                        pltpu.VMEM((NSA_KV_HEADS, NSA_HPG * rq, NSA_DH), F32),
                        pltpu.VMEM((LANE, 2 * KV_W), F32)])
    return pl.pallas_call(
        functools.partial(_nsa_sample_attend_kernel, past_len=past_len, page_rows=page_rows),
        grid_spec=grid_spec,
        out_shape=jax.ShapeDtypeStruct((batch * rq, qw), F32),
        compiler_params=_cparams("parallel", "arbitrary"),
    )(page_table, *([sel_pool] * PAGES_PER_STEP), proj, sel_t, e, proj, proj, win_kv, proj, o_c)


def _prep_w_in(w_in_l):
    main = w_in_l[:, :COL_NG]
    ng = w_in_l[:, COL_NG:COL_NG + 3 * NSA_HEADS].reshape(D_MODEL, 3, NSA_KV_HEADS, NSA_HPG)
    ng = ng.transpose(0, 2, 1, 3).reshape(D_MODEL, NSA_KV_HEADS, 3 * NSA_HPG)
    ng = jnp.pad(ng, [(0, 0), (0, 0), (0, LANE - 3 * NSA_HPG)]).reshape(D_MODEL, NSA_KV_HEADS * LANE)
    return jnp.concatenate([main, ng], axis=1).astype(BF16)


def _tile_mods(mod_rows, sub, tiles_per_seq=None, rows_per_seq=None):
    out = []
    for k in range(3):
        m = mod_rows[:, (sub * 3 + k) * D_MODEL:(sub * 3 + k + 1) * D_MODEL]
        if tiles_per_seq is not None:
            out.append(jnp.repeat(m, tiles_per_seq, axis=0)[:, None, :])
        else:
            out.append(jnp.repeat(m, rows_per_seq, axis=0)[None])
    return out


def kernel(x_prompt, x_sample, cache_cmp_kv, cache_sel_kv, cache_win_kv, state_ret, page_table, c_prompt, c_sample,
           norm_g, w_ada, b_ada, w_in, w_out, cmp_pe, cmp_w1, cmp_b1, cmp_w2, ffn_w_gate, ffn_w_up, ffn_w_down,
           final_g):
    bp, t, _ = x_prompt.shape
    bs, ts, _ = x_sample.shape
    win_buf = cache_win_kv.shape[2]
    n_pool, page_rows = cache_cmp_kv.shape[1:3]
    past_len = page_table.shape[1] * page_rows
    assert ts <= SAMPLE_ROWS and ts < CMP_STRIDE and past_len % CMP_STRIDE == 0
    assert page_table.shape[1] % PAGES_PER_STEP == 0 and past_len % SEL_BLOCK == 0
    n_sel_s = -(-(past_len + ts) // SEL_BLOCK)
    assert n_sel_s <= past_len // SEL_BLOCK + 1 and past_len // SEL_BLOCK + LANE <= SEL_ROWS_SAMPLE
    cmp_pool = cache_cmp_kv.reshape(DEPTH, n_pool, page_rows, 2 * KV_W)
    sel_pool = cache_sel_kv.reshape(DEPTH, n_pool, page_rows, 2 * KV_W)
    win_kv = cache_win_kv.reshape(DEPTH, bs, win_buf, 2 * KV_W)

    c_all = jnp.concatenate([c_prompt, c_sample], axis=0)
    c_all = jnp.pad(c_all, [(0, -c_all.shape[0] % 8), (0, 0)])
    mod = _modulation(c_all, w_ada, b_ada)

    bm = 512
    bm_p = 1024
    xp = x_prompt.reshape(bp * t, D_MODEL)
    pos_p = jnp.arange(t, dtype=jnp.int32)
    s0_p = jnp.zeros((bp, RET_HEADS, RET_DK, RET_DV), F32)
    rq = SAMPLE_ROWS
    rows_s = bs * rq
    xs = jnp.pad(x_sample, [(0, 0), (0, rq - ts), (0, 0)]).reshape(rows_s, D_MODEL)
    pos_s = past_len + jnp.arange(rq, dtype=jnp.int32)
    pc, ps, pw, pr, sc, ss, sw, sr = ([] for _ in range(8))
    for l in range(DEPTH):
        wg = ffn_w_gate[l].astype(BF16)
        wu = ffn_w_up[l].astype(BF16)
        wd = ffn_w_down[l].astype(BF16)
        w_in_l = _prep_w_in(w_in[l])
        w_out_l = w_out[l].astype(BF16)
        cmp_w = _compress_weights(cmp_pe[l], cmp_w1[l], cmp_b1[l], cmp_w2[l])
        mod_p = mod[l, :bp]

        m0 = _tile_mods(mod_p, 0, tiles_per_seq=t // bm)
        xp = _ffn(xp, norm_g[l, 0], m0, wg[0], wu[0], wd[0], bm)
        m1 = _tile_mods(mod_p, 1, tiles_per_seq=t // bm_p)
        proj = _proj(xp, norm_g[l, 1], m1, w_in_l, bm_p)
        y_ret, s_ret = _retention(proj, s0_p, pos_p, bp, t, RET_CHUNK, RET_CHUNK, BF16)
        kc = _compress_prompt(proj, cmp_w, bp, t)
        y_nsa = _nsa_prompt(proj, kc, bp, t)
        xp = _outproj(xp, y_ret, y_nsa, m1[2], w_out_l, bm_p)
        m2 = _tile_mods(mod_p, 2, tiles_per_seq=t // bm)
        xp = _ffn(xp, norm_g[l, 2], m2, wg[1], wu[1], wd[1], bm)

        kvshape = (bp, t, 2, NSA_KV_HEADS, NSA_DH)
        pc.append(proj[:, COL_KVC:COL_KVS].reshape(kvshape))
        ps.append(proj[:, COL_KVS:COL_KVW].reshape(kvshape))
        pw.append(proj[:, COL_KVW:COL_NG].reshape(kvshape)[:, t - win_buf:])
        pr.append(s_ret)

        mod_s = mod[l, bp:bp + bs]
        m0 = _tile_mods(mod_s, 0, rows_per_seq=rq)
        xs = _ffn(xs, norm_g[l, 0], m0, wg[0], wu[0], wd[0], rows_s)
        m1 = _tile_mods(mod_s, 1, rows_per_seq=rq)
        proj_s = _proj(xs, norm_g[l, 1], m1, w_in_l, rows_s)
        y_ret, s_ret = _retention(proj_s, state_ret[l], pos_s, bs, rq, ts, rq, F32)
        kc_s = _compress_sample(cmp_pool, l, page_table, cmp_w)
        o_c, sel_t = _nsa_sample_select(proj_s, kc_s, bs, past_len, n_sel_s)
        y_nsa = _nsa_sample_attend(proj_s, o_c, sel_t, sel_pool, win_kv, l, page_table, past_len)
        xs = _outproj(xs, y_ret, y_nsa, m1[2], w_out_l, rows_s)
        m2 = _tile_mods(mod_s, 2, rows_per_seq=rq)
        xs = _ffn(xs, norm_g[l, 2], m2, wg[1], wu[1], wd[1], rows_s)

        new_kv = lambda c0: proj_s[:, c0:c0 + 2 * KV_W].reshape(bs, rq, 2, NSA_KV_HEADS, NSA_DH)[:, :ts]
        sc.append(new_kv(COL_KVC))
        ss.append(new_kv(COL_KVS))
        sw.append(jnp.concatenate([cache_win_kv[l], new_kv(COL_KVW)], axis=1)[:, ts:])
        sr.append(s_ret)

    y_prompt = _final_norm(xp, final_g, bm).reshape(bp, t, D_MODEL)
    y_sample = _final_norm(xs, final_g, rows_s).reshape(bs, rq, D_MODEL)[:, :ts]
    return (y_prompt, y_sample, jnp.stack(pc), jnp.stack(ps), jnp.stack(pw), jnp.stack(pr),
            jnp.stack(sc), jnp.stack(ss), jnp.stack(sw), jnp.stack(sr))
```

```python
import functools

import jax
import jax.numpy as jnp
import numpy as np
from jax import lax
from jax.experimental import pallas as pl
from jax.experimental.pallas import tpu as pltpu

F32 = jnp.float32
BF16 = jnp.bfloat16

D_MODEL = 4096
DEPTH = 2
RET_HEADS = 8
RET_DK = 128
RET_DV = 256
RET_CHUNK = 128
ROPE_BASE = 10000.0
NSA_HEADS = 16
NSA_DH = 128
NSA_KV_HEADS = 4
NSA_HPG = NSA_HEADS // NSA_KV_HEADS
CMP_BLOCK = 32
CMP_STRIDE = 16
CMP_RATIO = CMP_BLOCK // CMP_STRIDE
CMP_HIDDEN = 2 * NSA_DH
SEL_BLOCK = 64
SEL_TOPK = 16
WINDOW = 512
N_SUB = 3
RMS_EPS = 1e-6
NEG_INF = -1e30
FORCE_SCORE = 1e9
KV_W = NSA_KV_HEADS * NSA_DH
ATT_SCALE = NSA_DH ** -0.5

COL_RQ = 0
COL_RK = COL_RQ + RET_HEADS * RET_DK
COL_RV = COL_RK + RET_HEADS * RET_DK
COL_RG = COL_RV + RET_HEADS * RET_DV
COL_NQ = COL_RG + RET_HEADS * RET_DV
COL_KVC = COL_NQ + NSA_HEADS * NSA_DH
COL_KVS = COL_KVC + 2 * KV_W
COL_KVW = COL_KVS + 2 * KV_W
COL_NG = COL_KVW + 2 * KV_W
N_PROJ = COL_NG + NSA_KV_HEADS * 128

LANE = 128
SAMPLE_ROWS = 8
VMEM_LIMIT_BYTES = 60 * 1024 * 1024


def _cparams(*sem):
    return pltpu.CompilerParams(dimension_semantics=sem, vmem_limit_bytes=VMEM_LIMIT_BYTES)


def _silu(x):
    return x * jax.nn.sigmoid(x)


def _dot(a, b):
    return jnp.dot(a, b, preferred_element_type=F32)


def _dot_nt(a, b):
    return lax.dot_general(a, b, (((1,), (1,)), ((), ())), preferred_element_type=F32)


def _masked_softmax(s, valid):
    s = jnp.where(valid, s, NEG_INF)
    m = jnp.max(s, axis=-1, keepdims=True)
    e = jnp.where(valid, jnp.exp(s - m), 0.0)
    return e / jnp.maximum(jnp.sum(e, axis=-1, keepdims=True), 1e-30)


def _mod_kernel(c_ref, w_ref, b_ref, o_ref):
    a = _silu(c_ref[...]).astype(BF16)
    o_ref[...] = _dot(a, w_ref[...].astype(BF16)) + b_ref[...]


def _modulation(c_all, w_ada, b_ada):
    rows = c_all.shape[0]
    n = w_ada.shape[-1]
    bn = 512
    return pl.pallas_call(
        _mod_kernel,
        grid=(DEPTH, n // bn),
        in_specs=[pl.BlockSpec((rows, D_MODEL), lambda l, j: (0, 0)),
                  pl.BlockSpec((None, D_MODEL, bn), lambda l, j: (l, 0, j)),
                  pl.BlockSpec((None, 1, bn), lambda l, j: (l, 0, j))],
        out_specs=pl.BlockSpec((None, rows, bn), lambda l, j: (l, 0, j)),
        out_shape=jax.ShapeDtypeStruct((DEPTH, rows, n), F32),
        compiler_params=_cparams("parallel", "parallel"),
    )(c_all, w_ada, b_ada.reshape(DEPTH, 1, n))


def _norm_modulate_into(h_ref, x_ref, g_ref, sh_ref, sc_ref):
    rows = x_ref.shape[0]
    per_row = sh_ref.shape[0] != 1
    step = min(rows, 128)
    for r0 in range(0, rows, step):
        x = x_ref[r0:r0 + step, :]
        y = x * lax.rsqrt(jnp.mean(x * x, axis=-1, keepdims=True) + RMS_EPS) * g_ref[...]
        sc = sc_ref[r0:r0 + step, :] if per_row else sc_ref[...]
        sh = sh_ref[r0:r0 + step, :] if per_row else sh_ref[...]
        h_ref[r0:r0 + step, :] = (y * (1.0 + sc) + sh).astype(h_ref.dtype)


def _mod_spec(mod_rows, bm):
    return pl.BlockSpec((None, mod_rows, D_MODEL), lambda i, j: (i, 0, 0))


def _ffn_kernel(x_ref, g_ref, sh_ref, sc_ref, gt_ref, wg_ref, wu_ref, wd_ref, o_ref, h_ref, *, n_chunk):
    j = pl.program_id(1)

    @pl.when(j == 0)
    def _():
        _norm_modulate_into(h_ref, x_ref, g_ref, sh_ref, sc_ref)
        o_ref[...] = jnp.zeros_like(o_ref)

    h = h_ref[...]
    a = (_silu(_dot(h, wg_ref[...])) * _dot(h, wu_ref[...])).astype(BF16)
    for n0 in range(0, D_MODEL, n_chunk):
        o_ref[:, n0:n0 + n_chunk] += _dot(a, wd_ref[:, n0:n0 + n_chunk])

    @pl.when(j == pl.num_programs(1) - 1)
    def _():
        rows = x_ref.shape[0]
        per_row = gt_ref.shape[0] != 1
        step = min(rows, 128)
        for r0 in range(0, rows, step):
            gt = gt_ref[r0:r0 + step, :] if per_row else gt_ref[...]
            o_ref[r0:r0 + step, :] = x_ref[r0:r0 + step, :] + (0.5 * gt) * o_ref[r0:r0 + step, :]


def _ffn(x, g, mods, wg, wu, wd, layer, half, bm):
    rows = x.shape[0]
    d_ff = wg.shape[-1]
    tf = 256
    shift, scale, gate = mods
    mr = shift.shape[1]
    return pl.pallas_call(
        functools.partial(_ffn_kernel, n_chunk=512),
        grid=(rows // bm, d_ff // tf),
        in_specs=[pl.BlockSpec((bm, D_MODEL), lambda i, j: (i, 0)),
                  pl.BlockSpec((1, D_MODEL), lambda i, j: (0, 0)),
                  _mod_spec(mr, bm), _mod_spec(mr, bm), _mod_spec(mr, bm),
                  pl.BlockSpec((None, None, D_MODEL, tf), lambda i, j: (layer, half, 0, j)),
                  pl.BlockSpec((None, None, D_MODEL, tf), lambda i, j: (layer, half, 0, j)),
                  pl.BlockSpec((None, None, tf, D_MODEL), lambda i, j: (layer, half, j, 0))],
        out_specs=pl.BlockSpec((bm, D_MODEL), lambda i, j: (i, 0)),
        out_shape=jax.ShapeDtypeStruct((rows, D_MODEL), F32),
        scratch_shapes=[pltpu.VMEM((bm, D_MODEL), BF16)],
        compiler_params=_cparams("parallel", "arbitrary"),
    )(x, g.reshape(1, D_MODEL), shift, scale, gate, wg, wu, wd)


def _proj_kernel(x_ref, g_ref, sh_ref, sc_ref, w_ref, wgate_ref, o_ref, h_ref, *, n_main):
    j = pl.program_id(1)

    @pl.when(j == 0)
    def _():
        _norm_modulate_into(h_ref, x_ref, g_ref, sh_ref, sc_ref)

    @pl.when(j < n_main)
    def _():
        o_ref[...] = _dot(h_ref[...], w_ref[...])

    @pl.when(j >= n_main)
    def _():
        o_ref[...] = _dot(h_ref[...], wgate_ref[...])


def _proj(x, g, mods, w_main, w_gate, layer, bm):
    rows = x.shape[0]
    bn = N_PROJ - COL_NG
    n_main = COL_NG // bn
    shift, scale, _ = mods
    mr = shift.shape[1]
    return pl.pallas_call(
        functools.partial(_proj_kernel, n_main=n_main),
        grid=(rows // bm, n_main + 1),
        in_specs=[pl.BlockSpec((bm, D_MODEL), lambda i, j: (i, 0)),
                  pl.BlockSpec((1, D_MODEL), lambda i, j: (0, 0)),
                  _mod_spec(mr, bm), _mod_spec(mr, bm),
                  pl.BlockSpec((None, D_MODEL, bn), lambda i, j: (layer, 0, jnp.minimum(j, n_main - 1))),
                  pl.BlockSpec((None, D_MODEL, bn), lambda i, j: (layer, 0, 0))],
        out_specs=pl.BlockSpec((bm, bn), lambda i, j: (i, j)),
        out_shape=jax.ShapeDtypeStruct((rows, N_PROJ), F32),
        scratch_shapes=[pltpu.VMEM((bm, D_MODEL), BF16)],
        compiler_params=_cparams("parallel", "arbitrary"),
    )(x, g.reshape(1, D_MODEL), shift, scale, w_main, w_gate)


def _outproj_kernel(x_ref, yr_ref, yn_ref, gt_ref, wa_ref, wb_ref, o_ref):
    y = _dot(yr_ref[...].astype(BF16), wa_ref[...]) + _dot(yn_ref[...].astype(BF16), wb_ref[...])
    o_ref[...] = x_ref[...] + gt_ref[...] * y


def _outproj(x, y_ret, y_nsa, gate, w_out, layer, bm):
    rows = x.shape[0]
    half = y_ret.shape[1]
    bn = 512
    mr = gate.shape[1]
    return pl.pallas_call(
        _outproj_kernel,
        grid=(rows // bm, D_MODEL // bn),
        in_specs=[pl.BlockSpec((bm, bn), lambda i, j: (i, j)),
                  pl.BlockSpec((bm, half), lambda i, j: (i, 0)),
                  pl.BlockSpec((bm, half), lambda i, j: (i, 0)),
                  pl.BlockSpec((None, mr, bn), lambda i, j: (i, 0, j)),
                  pl.BlockSpec((None, half, bn), lambda i, j: (layer, 0, j)),
                  pl.BlockSpec((None, half, bn), lambda i, j: (layer, 1, j))],
        out_specs=pl.BlockSpec((bm, bn), lambda i, j: (i, j)),
        out_shape=jax.ShapeDtypeStruct((rows, D_MODEL), F32),
        compiler_params=_cparams("parallel", "arbitrary"),
    )(x, y_ret, y_nsa, gate, w_out, w_out)


def _final_norm_kernel(x_ref, g_ref, o_ref):
    x = x_ref[...]
    o_ref[...] = x * lax.rsqrt(jnp.mean(x * x, axis=-1, keepdims=True) + RMS_EPS) * g_ref[...]


def _final_norm(x, g, bm):
    rows = x.shape[0]
    return pl.pallas_call(
        _final_norm_kernel,
        grid=(rows // bm,),
        in_specs=[pl.BlockSpec((bm, D_MODEL), lambda i: (i, 0)),
                  pl.BlockSpec((1, D_MODEL), lambda i: (0, 0))],
        out_specs=pl.BlockSpec((bm, D_MODEL), lambda i: (i, 0)),
        out_shape=jax.ShapeDtypeStruct((rows, D_MODEL), F32),
        compiler_params=_cparams("parallel"),
    )(x, g.reshape(1, D_MODEL))


def _retention_kernel(q_ref, k_ref, v_ref, rg_ref, cos_ref, sin_ref, dm_ref, qd_ref, kd_ref, cd_ref, s0_ref,
                      y_ref, s_ref, *pad, cq):
    @pl.when(pl.program_id(1) == 0)
    def _():
        s_ref[...] = s0_ref[...]

    cos = cos_ref[...]
    sin = sin_ref[...]
    padded = cq != RET_CHUNK
    if padded:
        kp_ref, vp_ref = pad
        kp_ref[...] = jnp.zeros_like(kp_ref)
        vp_ref[...] = jnp.zeros_like(vp_ref)
    for h in range(RET_HEADS):
        q = q_ref[:, h * RET_DK:(h + 1) * RET_DK]
        k = k_ref[:, h * RET_DK:(h + 1) * RET_DK]
        q = q * cos + pltpu.roll(q, RET_DK // 2, 1) * sin
        k = (k * cos + pltpu.roll(k, RET_DK // 2, 1) * sin) * (RET_DK ** -0.5)
        v = v_ref[:, h * RET_DV:(h + 1) * RET_DV]
        if padded:
            kp_ref[0:cq, :] = k
            vp_ref[0:cq, :] = v
            k = kp_ref[...]
            v = vp_ref[...]
        qb = q.astype(BF16)
        vb = v.astype(BF16)
        att = _dot_nt(qb, k.astype(BF16)) * dm_ref[h]
        s = s_ref[h]
        o = _dot(att.astype(BF16), vb) + _dot(qb, s.astype(BF16)) * qd_ref[h]
        kd = (k * kd_ref[h]).T.astype(BF16)
        s_ref[h] = s * cd_ref[h] + _dot(kd, vb)
        o = o * lax.rsqrt(jnp.mean(o * o, axis=-1, keepdims=True) + RMS_EPS)
        y = _silu(rg_ref[:, h * RET_DV:(h + 1) * RET_DV]) * o
        y_ref[:, h * RET_DV:(h + 1) * RET_DV] = y.astype(y_ref.dtype)


def _retention_tables(pos, chunk, cq):
    half = RET_DK // 2
    inv = jnp.power(ROPE_BASE, -jnp.arange(half, dtype=F32) * 2.0 / RET_DK)
    ang = pos.astype(F32)[:, None] * inv[None, :]
    cos = jnp.cos(ang)
    sin = jnp.sin(ang)
    cos2 = jnp.concatenate([cos, cos], axis=-1)
    sin2 = jnp.concatenate([-sin, sin], axis=-1)
    log_g = jnp.log(1.0 - jnp.exp2(-5.0 - jnp.arange(RET_HEADS, dtype=F32)))
    i = jnp.arange(chunk, dtype=F32)
    diff = i[:, None] - i[None, :]
    dmask = jnp.where(diff >= 0, jnp.exp(log_g[:, None, None] * jnp.maximum(diff, 0.0)), 0.0)
    q_dec = jnp.exp(log_g[None, :] * (i[:, None] + 1.0))
    k_dec = jnp.exp(log_g[None, :] * (chunk - 1.0 - i[:, None]))
    c_dec = jnp.exp(log_g * chunk)
    dm = jnp.zeros((RET_HEADS, cq, RET_CHUNK), F32).at[:, :chunk, :chunk].set(dmask)
    qd = jnp.zeros((RET_HEADS, cq, RET_DV), F32).at[:, :chunk, :].set(
        jnp.broadcast_to(q_dec.T[:, :, None], (RET_HEADS, chunk, RET_DV)))
    kd = jnp.zeros((RET_HEADS, RET_CHUNK, RET_DK), F32).at[:, :chunk, :].set(
        jnp.broadcast_to(k_dec.T[:, :, None], (RET_HEADS, chunk, RET_DK)))
    cd = jnp.broadcast_to(c_dec[:, None, None], (RET_HEADS, 1, RET_DV))
    return cos2, sin2, dm, qd, kd, cd


def _retention(proj, s0, pos, batch, t_rows, chunk, cq, out_dtype):
    nc = t_rows // cq
    cos2, sin2, dm, qd, kd, cd = _retention_tables(pos, chunk, cq)
    hk = RET_HEADS * RET_DK
    hv = RET_HEADS * RET_DV
    row = lambda b, c: b * nc + c
    scratch = [] if cq == RET_CHUNK else [pltpu.VMEM((RET_CHUNK, RET_DK), F32), pltpu.VMEM((RET_CHUNK, RET_DV), F32)]
    full3 = lambda shape: pl.BlockSpec(shape, lambda b, c: (0, 0, 0))
    return pl.pallas_call(
        functools.partial(_retention_kernel, cq=cq),
        grid=(batch, nc),
        in_specs=[pl.BlockSpec((cq, hk), lambda b, c: (row(b, c), COL_RQ // hk)),
                  pl.BlockSpec((cq, hk), lambda b, c: (row(b, c), COL_RK // hk)),
                  pl.BlockSpec((cq, hv), lambda b, c: (row(b, c), COL_RV // hv)),
                  pl.BlockSpec((cq, hv), lambda b, c: (row(b, c), COL_RG // hv)),
                  pl.BlockSpec((cq, RET_DK), lambda b, c: (c, 0)),
                  pl.BlockSpec((cq, RET_DK), lambda b, c: (c, 0)),
                  full3(dm.shape), full3(qd.shape), full3(kd.shape), full3(cd.shape),
                  pl.BlockSpec((None, RET_HEADS, RET_DK, RET_DV), lambda b, c: (b, 0, 0, 0))],
        out_specs=[pl.BlockSpec((cq, hv), lambda b, c: (row(b, c), 0)),
                   pl.BlockSpec((None, RET_HEADS, RET_DK, RET_DV), lambda b, c: (b, 0, 0, 0))],
        out_shape=[jax.ShapeDtypeStruct((batch * t_rows, hv), out_dtype),
                   jax.ShapeDtypeStruct((batch, RET_HEADS, RET_DK, RET_DV), F32)],
        scratch_shapes=scratch,
        compiler_params=_cparams("parallel", "arbitrary"),
    )(proj, proj, proj, proj, cos2, sin2, dm, qd, kd, cd, s0)


def _kv_rows(ref, r0, rows, c, g):
    if len(ref.shape) == 2:
        col = (c * NSA_KV_HEADS + g) * NSA_DH
        return ref[r0:r0 + rows, col:col + NSA_DH]
    return ref[r0:r0 + rows, c, g, :]


def _compress_kernel(*refs, n_src):
    srcs = refs[:n_src]
    perm_ref, w1_ref, pe_ref, b1_ref, w2_ref, o_ref, x_ref, prev_ref = refs[n_src:]
    grp = perm_ref.shape[0]
    segs = grp // CMP_STRIDE
    n_grp = srcs[0].shape[0] // grp
    s_tot = n_src * n_grp * segs
    rows = NSA_KV_HEADS * s_tot

    @pl.when(pl.program_id(1) == 0)
    def _():
        prev_ref[...] = jnp.zeros_like(prev_ref)

    for si, src in enumerate(srcs):
        for gi in range(n_grp):
            r0 = (si * n_grp + gi) * segs
            for c in range(2):
                for g in range(NSA_KV_HEADS):
                    rows_in = _kv_rows(src, gi * grp, grp, c, g)
                    blk = _dot(perm_ref[...], rows_in.astype(BF16))
                    for s in range(CMP_STRIDE):
                        x_ref[c, g * s_tot + r0:g * s_tot + r0 + segs, s * NSA_DH:(s + 1) * NSA_DH] = (
                            blk[s * segs:(s + 1) * segs, :])

    first = (lax.broadcasted_iota(jnp.int32, (rows, CMP_HIDDEN), 0) & (s_tot - 1)) == 0
    for c in range(2):
        w1 = w1_ref[c]
        part = _dot(x_ref[c].astype(BF16), w1)
        pe_part = _dot(pe_ref[c], w1)
        bias = b1_ref[c] + pe_part[0:1, :CMP_HIDDEN] + pe_part[1:2, CMP_HIDDEN:]
        p0 = part[:, :CMP_HIDDEN]
        p1 = part[:, CMP_HIDDEN:]
        shifted = jnp.where(first, pltpu.roll(prev_ref[c], rows - s_tot + 1, 0), pltpu.roll(p0, 1, 0))
        prev_ref[c] = p0
        kc = _dot(_silu(bias + shifted + p1).astype(BF16), w2_ref[c])
        for g in range(NSA_KV_HEADS):
            col = (c * NSA_KV_HEADS + g) * NSA_DH
            o_ref[:, col:col + NSA_DH] = kc[g * s_tot:(g + 1) * s_tot].astype(o_ref.dtype)


def _compress_weights(cmp_pe, cmp_w1, cmp_b1, cmp_w2):
    w1 = cmp_w1.reshape(2, CMP_RATIO, CMP_STRIDE, NSA_DH, CMP_HIDDEN).transpose(0, 2, 3, 1, 4)
    w1 = w1.reshape(2, CMP_STRIDE * NSA_DH, CMP_RATIO * CMP_HIDDEN).astype(BF16)
    pe = cmp_pe.reshape(2, CMP_RATIO, CMP_STRIDE * NSA_DH)
    pe = jnp.pad(pe, [(0, 0), (0, 8 - CMP_RATIO), (0, 0)]).astype(BF16)
    return _segment_permutation(), w1, pe, cmp_b1.reshape(2, 1, CMP_HIDDEN), cmp_w2.astype(BF16)


COMPRESS_GROUP_ROWS = 128


def _segment_permutation():
    segs = COMPRESS_GROUP_ROWS // CMP_STRIDE
    r = np.arange(COMPRESS_GROUP_ROWS)
    src = (r % segs) * CMP_STRIDE + r // segs
    return jnp.asarray(src[:, None] == np.arange(COMPRESS_GROUP_ROWS)[None, :], BF16)


def _compress_specs(idx):
    kdim = CMP_STRIDE * NSA_DH
    return [pl.BlockSpec((COMPRESS_GROUP_ROWS, COMPRESS_GROUP_ROWS), idx(lambda *_: (0, 0))),
            pl.BlockSpec((2, kdim, CMP_RATIO * CMP_HIDDEN), idx(lambda *_: (0, 0, 0))),
            pl.BlockSpec((2, 8, kdim), idx(lambda *_: (0, 0, 0))),
            pl.BlockSpec((2, 1, CMP_HIDDEN), idx(lambda *_: (0, 0, 0))),
            pl.BlockSpec((2, CMP_HIDDEN, NSA_DH), idx(lambda *_: (0, 0, 0)))]


def _compress_scratch(s_tot):
    rows = NSA_KV_HEADS * s_tot
    return [pltpu.VMEM((2, rows, CMP_STRIDE * NSA_DH), F32), pltpu.VMEM((2, rows, CMP_HIDDEN), F32)]


def _compress_prompt(proj, cmp_w, batch, t):
    segs = t // CMP_STRIDE
    return pl.pallas_call(
        functools.partial(_compress_kernel, n_src=1),
        grid=(batch, 1),
        in_specs=[pl.BlockSpec((t, 2 * KV_W), lambda b, i: (b, COL_KVC // (2 * KV_W)))]
        + _compress_specs(lambda f: f),
        out_specs=pl.BlockSpec((None, segs, 2 * KV_W), lambda b, i: (b, 0, 0)),
        out_shape=jax.ShapeDtypeStruct((batch, segs, 2 * KV_W), BF16),
        scratch_shapes=_compress_scratch(segs),
        compiler_params=_cparams("parallel", "arbitrary"),
    )(proj, *cmp_w)


def _split_bf16(x):
    hi = x.astype(BF16)
    return hi, (x - hi.astype(F32)).astype(BF16)


def _selection_scores(psl_t, t_pos):
    j = lax.broadcasted_iota(jnp.int32, psl_t.shape, 0)
    jc = t_pos // SEL_BLOCK
    forced = (j == 0) | (j == jc) | (j == jc - 1)
    return jnp.where(j > jc, NEG_INF, jnp.where(forced, FORCE_SCORE, psl_t))


def _beats(row, jp, score, j):
    return jnp.where(row > score, 1.0, jnp.where(row == score, jnp.where(j > jp, 1.0, 0.0), 0.0))


def _nsa_prompt_kernel(q_ref, gt_ref, kck_ref, kcv_ref, ks_ref, vs_ref, kw_ref, vw_ref, at_ref, e_ref, cb_ref, wb_ref,
                       o_ref, ksb, vsb, kwb, vwb, m_ref, l_ref, acc_ref, *, tq, tk, n_sel):
    qi = pl.program_id(2)
    rows = NSA_HPG * tq

    @pl.when(qi == 0)
    def _():
        ksb[...] = ks_ref[...].astype(BF16)
        vsb[...] = vs_ref[...].astype(BF16)
        kwb[...] = kw_ref[...].astype(BF16)
        vwb[...] = vw_ref[...].astype(BF16)

    def t_of(shape):
        return qi * tq + (lax.broadcasted_iota(jnp.int32, shape, 0) & (tq - 1))

    q4 = (jnp.concatenate([q_ref[:, p * NSA_DH:(p + 1) * NSA_DH] for p in range(NSA_HPG)], axis=0)
          * ATT_SCALE).astype(BF16)

    n_slot = kck_ref.shape[0]
    slot = lax.broadcasted_iota(jnp.int32, (rows, n_slot), 1)
    valid = (slot >= 1) & (slot * CMP_STRIDE + (CMP_BLOCK - CMP_STRIDE - 1) <= t_of((rows, n_slot)))
    p_c = _masked_softmax(_dot_nt(q4, kck_ref[...]), valid)
    o_c = _dot(p_c.astype(BF16), kcv_ref[...])

    pg = p_c[0:tq]
    for p in range(1, NSA_HPG):
        pg = pg + p_c[p * tq:(p + 1) * tq]
    hi, lo = _split_bf16(pg)
    psl_t = _dot_nt(at_ref[...], hi) + _dot_nt(at_ref[...], lo)
    jshape = psl_t.shape
    j = lax.broadcasted_iota(jnp.int32, jshape, 0)
    score = _selection_scores(psl_t, qi * tq + lax.broadcasted_iota(jnp.int32, jshape, 1))
    rank = jnp.zeros(jshape, F32)
    for jp in range(n_sel):
        rank = rank + _beats(score[jp:jp + 1, :], jp, score, j)
    bias_t = jnp.where((rank < min(SEL_TOPK, n_sel)) & (j < n_sel), 0.0, NEG_INF)
    bias_t = jnp.concatenate([bias_t, jnp.full((e_ref.shape[1] - jshape[0], tq), NEG_INF, F32)], axis=0).astype(BF16)
    eye = jnp.where(lax.broadcasted_iota(jnp.int32, (tq, tq), 0) == lax.broadcasted_iota(jnp.int32, (tq, tq), 1),
                    1.0, 0.0).astype(BF16)
    sel_bias = _dot_nt(eye, bias_t).astype(BF16)

    m_ref[...] = jnp.full(m_ref.shape, NEG_INF, F32)
    l_ref[...] = jnp.zeros(l_ref.shape, F32)
    acc_ref[...] = jnp.zeros(acc_ref.shape, F32)

    def chunk(c, causal_bias):
        k0 = pl.multiple_of(c * tk, tk)
        bias = _dot(sel_bias, e_ref[c])
        if causal_bias is not None:
            bias = bias + causal_bias
        s = _dot_nt(q4, ksb[pl.ds(k0, tk), :]) + jnp.concatenate([bias] * NSA_HPG, axis=0)
        m_old = m_ref[...]
        m_new = jnp.maximum(m_old, jnp.max(s, axis=-1, keepdims=True))
        alpha = jnp.exp(m_old - m_new)
        e = jnp.exp(s - m_new)
        l_ref[...] = l_ref[...] * alpha + jnp.sum(e, axis=-1, keepdims=True)
        acc_ref[...] = acc_ref[...] * alpha + _dot(e.astype(BF16), vsb[pl.ds(k0, tk), :])
        m_ref[...] = m_new

    n_full = (qi * tq) // tk

    def full_chunk(c, carry):
        chunk(c, None)
        return carry

    lax.fori_loop(0, n_full, full_chunk, 0)
    chunk(n_full, cb_ref[qi % (tk // tq)])
    o_s = acc_ref[...] / jnp.maximum(l_ref[...], 1e-30)

    n_back = WINDOW // tq
    w_len = wb_ref.shape[2]
    qw_i = jnp.minimum(qi, n_back)
    w0 = pl.multiple_of((qi - qw_i) * tq, tq)
    s_w = _dot_nt(q4, kwb[pl.ds(w0, w_len), :]) + jnp.concatenate([wb_ref[qw_i]] * NSA_HPG, axis=0)
    e_w = jnp.exp(s_w - jnp.max(s_w, axis=-1, keepdims=True))
    p_w = e_w / jnp.maximum(jnp.sum(e_w, axis=-1, keepdims=True), 1e-30)
    o_w = _dot(p_w.astype(BF16), vwb[pl.ds(w0, w_len), :])

    gt = jax.nn.sigmoid(gt_ref[...])
    for p in range(NSA_HPG):
        r = slice(p * tq, (p + 1) * tq)
        y = (gt[:, p:p + 1] * o_c[r] + gt[:, NSA_HPG + p:NSA_HPG + p + 1] * o_s[r]
             + gt[:, 2 * NSA_HPG + p:2 * NSA_HPG + p + 1] * o_w[r])
        o_ref[:, p * NSA_DH:(p + 1) * NSA_DH] = y.astype(o_ref.dtype)


def _selection_sum_matrix(n_rows, n_slot, n_sel):
    ratio = SEL_BLOCK // CMP_STRIDE
    j = np.arange(n_rows)[:, None]
    m = np.arange(n_slot)[None, :]
    a = (j < n_sel) & (m >= ratio * j) & (m <= ratio * j + ratio + CMP_RATIO - 2)
    return jnp.asarray(a, BF16)


def _block_expand_matrix(n_chunks, n_rows, tk, blocks_per_chunk):
    c = np.arange(n_chunks)[:, None, None]
    j = np.arange(n_rows)[None, :, None]
    k = np.arange(tk)[None, None, :]
    return jnp.asarray((c * blocks_per_chunk + k // SEL_BLOCK) == j, BF16)


def _causal_bias_table(tq, tk):
    d = np.arange(tk // tq)[:, None, None]
    r = np.arange(tq)[None, :, None]
    k = np.arange(tk)[None, None, :]
    return jnp.asarray(np.where(k <= d * tq + r, 0.0, NEG_INF), F32)


def _window_bias_table(tq, n_back, w_len):
    d = np.arange(n_back + 1)[:, None, None]
    r = np.arange(tq)[None, :, None]
    k = np.arange(w_len)[None, None, :]
    rel = k - d * tq - r
    return jnp.asarray(np.where((rel <= 0) & (rel > -WINDOW), 0.0, NEG_INF), F32)


def _nsa_prompt(proj, kc, batch, t):
    tq, tk = 128, 512
    nq = t // tq
    n_sel = t // SEL_BLOCK
    n_slot = kc.shape[1]
    n_back = WINDOW // tq
    a_t = _selection_sum_matrix(-(-n_sel // 8) * 8, n_slot, n_sel)
    e = _block_expand_matrix(t // tk, LANE, tk, tk // SEL_BLOCK)
    cb = _causal_bias_table(tq, tk)
    wb = _window_bias_table(tq, n_back, min((n_back + 1) * tq, t))
    qw = NSA_HPG * NSA_DH
    kv = lambda base, c: pl.BlockSpec((t, NSA_DH), lambda b, g, qi: (b, (base + c * KV_W) // NSA_DH + g))
    return pl.pallas_call(
        functools.partial(_nsa_prompt_kernel, tq=tq, tk=tk, n_sel=n_sel),
        grid=(batch, NSA_KV_HEADS, nq),
        in_specs=[pl.BlockSpec((tq, qw), lambda b, g, qi: (b * nq + qi, COL_NQ // qw + g)),
                  pl.BlockSpec((tq, LANE), lambda b, g, qi: (b * nq + qi, COL_NG // LANE + g)),
                  pl.BlockSpec((None, n_slot, NSA_DH), lambda b, g, qi: (b, 0, g)),
                  pl.BlockSpec((None, n_slot, NSA_DH), lambda b, g, qi: (b, 0, NSA_KV_HEADS + g)),
                  kv(COL_KVS, 0), kv(COL_KVS, 1), kv(COL_KVW, 0), kv(COL_KVW, 1),
                  pl.BlockSpec(a_t.shape, lambda b, g, qi: (0, 0)),
                  pl.BlockSpec(e.shape, lambda b, g, qi: (0, 0, 0)),
                  pl.BlockSpec(cb.shape, lambda b, g, qi: (0, 0, 0)),
                  pl.BlockSpec(wb.shape, lambda b, g, qi: (0, 0, 0))],
        out_specs=pl.BlockSpec((tq, qw), lambda b, g, qi: (b * nq + qi, g)),
        out_shape=jax.ShapeDtypeStruct((batch * t, NSA_HEADS * NSA_DH), BF16),
        scratch_shapes=[pltpu.VMEM((t, NSA_DH), BF16)] * 4
        + [pltpu.VMEM((NSA_HPG * tq, 1), F32), pltpu.VMEM((NSA_HPG * tq, 1), F32),
           pltpu.VMEM((NSA_HPG * tq, NSA_DH), F32)],
        compiler_params=_cparams("parallel", "parallel", "arbitrary"),
    )(proj, proj, kc, kc, proj, proj, proj, proj, a_t, e, cb, wb)


PAGES_PER_STEP = 8


def _page_specs(layer, page_rows):
    def spec(k):
        return pl.BlockSpec((None, None, page_rows, 2, NSA_KV_HEADS, NSA_DH),
                            lambda b, i, pt: (layer, pt[b, i * PAGES_PER_STEP + k], 0, 0, 0, 0))
    return [spec(k) for k in range(PAGES_PER_STEP)]


def _compress_sample_kernel(pt_ref, *refs):
    _compress_kernel(*refs, n_src=PAGES_PER_STEP)


def _compress_sample(pool, layer, page_table, cmp_w):
    batch, n_pages = page_table.shape
    page_rows = pool.shape[2]
    segs = page_rows // CMP_STRIDE
    s_tot = PAGES_PER_STEP * segs
    steps = n_pages // PAGES_PER_STEP
    grid_spec = pltpu.PrefetchScalarGridSpec(
        num_scalar_prefetch=1, grid=(batch, steps),
        in_specs=_page_specs(layer, page_rows) + _compress_specs(lambda f: f),
        out_specs=pl.BlockSpec((None, s_tot, 2 * KV_W), lambda b, i, pt: (b, i, 0)),
        scratch_shapes=_compress_scratch(s_tot))
    return pl.pallas_call(
        _compress_sample_kernel,
        grid_spec=grid_spec,
        out_shape=jax.ShapeDtypeStruct((batch, steps * s_tot, 2 * KV_W), BF16),
        compiler_params=_cparams("parallel", "arbitrary"),
    )(page_table, *([pool] * PAGES_PER_STEP), *cmp_w)


def _stack_heads(q_ref, g):
    return jnp.concatenate([q_ref[:, (g * NSA_HPG + p) * NSA_DH:(g * NSA_HPG + p + 1) * NSA_DH]
                            for p in range(NSA_HPG)], axis=0).astype(BF16)


def _nsa_sample_select_kernel(q_ref, kck_ref, kcv_ref, at_ref, oc_ref, selt_ref, pg_ref, sc_ref, rank_ref, *,
                              past_len, n_sel):
    rq = SAMPLE_ROWS
    rows = NSA_HPG * rq
    n_slot = kck_ref.shape[0]
    pg_ref[...] = jnp.zeros_like(pg_ref)
    slot = lax.broadcasted_iota(jnp.int32, (rows, n_slot), 1)
    t_q = past_len + (lax.broadcasted_iota(jnp.int32, (rows, n_slot), 0) & (rq - 1))
    valid = (slot >= 1) & (slot * CMP_STRIDE + (CMP_BLOCK - CMP_STRIDE - 1) <= t_q)
    for g in range(NSA_KV_HEADS):
        q4 = _stack_heads(q_ref, g)
        p_c = _masked_softmax(_dot_nt(q4, kck_ref[:, g * NSA_DH:(g + 1) * NSA_DH]) * ATT_SCALE, valid)
        o_c = _dot(p_c.astype(BF16), kcv_ref[:, g * NSA_DH:(g + 1) * NSA_DH])
        pg = p_c[0:rq]
        for p in range(NSA_HPG):
            col = (g * NSA_HPG + p) * NSA_DH
            oc_ref[:, col:col + NSA_DH] = o_c[p * rq:(p + 1) * rq]
            if p:
                pg = pg + p_c[p * rq:(p + 1) * rq]
        pg_ref[g * rq:(g + 1) * rq, :] = pg
    hi, lo = _split_bf16(pg_ref[...])
    psl_t = _dot_nt(at_ref[...], hi) + _dot_nt(at_ref[...], lo)
    jshape = psl_t.shape
    t_pos = past_len + (lax.broadcasted_iota(jnp.int32, jshape, 1) & (rq - 1))
    sc_ref[...] = _selection_scores(psl_t, t_pos)
    rank_ref[...] = jnp.zeros_like(rank_ref)

    def body(jp, carry):
        j = lax.broadcasted_iota(jnp.int32, jshape, 0)
        rank_ref[...] += _beats(sc_ref[pl.ds(jp, 1), :], jp, sc_ref[...], j)
        return carry

    lax.fori_loop(0, n_sel, body, 0)
    selt_ref[...] = jnp.where(rank_ref[...] < min(SEL_TOPK, n_sel), 1.0, 0.0).astype(selt_ref.dtype)


SEL_ROWS_SAMPLE = 384


def _nsa_sample_select(proj, kc, batch, past_len, n_sel):
    n_slot = kc.shape[1]
    a_t = _selection_sum_matrix(SEL_ROWS_SAMPLE, n_slot, n_sel)
    qw = NSA_HEADS * NSA_DH
    return pl.pallas_call(
        functools.partial(_nsa_sample_select_kernel, past_len=past_len, n_sel=n_sel),
        grid=(batch,),
        in_specs=[pl.BlockSpec((SAMPLE_ROWS, qw), lambda b: (b, COL_NQ // qw)),
                  pl.BlockSpec((None, n_slot, KV_W), lambda b: (b, 0, 0)),
                  pl.BlockSpec((None, n_slot, KV_W), lambda b: (b, 0, 1)),
                  pl.BlockSpec(a_t.shape, lambda b: (0, 0))],
        out_specs=[pl.BlockSpec((SAMPLE_ROWS, qw), lambda b: (b, 0)),
                   pl.BlockSpec((None, SEL_ROWS_SAMPLE, LANE), lambda b: (b, 0, 0))],
        out_shape=[jax.ShapeDtypeStruct((batch * SAMPLE_ROWS, qw), F32),
                   jax.ShapeDtypeStruct((batch, SEL_ROWS_SAMPLE, LANE), BF16)],
        scratch_shapes=[pltpu.VMEM((LANE, n_slot), F32), pltpu.VMEM((SEL_ROWS_SAMPLE, LANE), F32),
                        pltpu.VMEM((SEL_ROWS_SAMPLE, LANE), F32)],
        compiler_params=_cparams("parallel"),
    )(proj, kc, kc, a_t)


def _nsa_sample_attend_kernel(pt_ref, *refs, past_len, page_rows):
    pages = refs[:PAGES_PER_STEP]
    (q_ref, selt_ref, e_ref, ksn_ref, kwn_ref, win_ref, gt_ref, oc_ref, o_ref,
     m_ref, l_ref, acc_ref, pad_ref) = refs[PAGES_PER_STEP:]
    i = pl.program_id(1)
    rq = SAMPLE_ROWS
    rows = NSA_HPG * rq
    blocks_per_step = PAGES_PER_STEP * page_rows // SEL_BLOCK

    @pl.when(i == 0)
    def _():
        m_ref[...] = jnp.full(m_ref.shape, NEG_INF, F32)
        l_ref[...] = jnp.zeros(l_ref.shape, F32)
        acc_ref[...] = jnp.zeros(acc_ref.shape, F32)

    def t_of(shape):
        return past_len + (lax.broadcasted_iota(jnp.int32, shape, 0) & (rq - 1))

    def online_update(g, s, ok, v):
        s = jnp.where(ok, s, NEG_INF)
        m_old = m_ref[g]
        m_new = jnp.maximum(m_old, jnp.max(s, axis=-1, keepdims=True))
        alpha = jnp.exp(m_old - m_new)
        e = jnp.where(ok, jnp.exp(s - m_new), 0.0)
        l_ref[g] = l_ref[g] * alpha + jnp.sum(e, axis=-1, keepdims=True)
        acc_ref[g] = acc_ref[g] * alpha + _dot(e.astype(BF16), v)
        m_ref[g] = m_new

    eye = jnp.where(lax.broadcasted_iota(jnp.int32, (LANE, LANE), 0) == lax.broadcasted_iota(jnp.int32, (LANE, LANE), 1),
                    1.0, 0.0).astype(BF16)

    j0 = pl.multiple_of(i * blocks_per_step, blocks_per_step)
    sel = _dot_nt(eye, selt_ref[pl.ds(j0, LANE), :]).astype(BF16)
    picked = _dot(sel, e_ref[...])
    for g in range(NSA_KV_HEADS):
        kg = jnp.concatenate([pg[:, 0, g, :] for pg in pages], axis=0).astype(BF16)
        vg = jnp.concatenate([pg[:, 1, g, :] for pg in pages], axis=0).astype(BF16)
        ok = jnp.concatenate([picked[g * rq:(g + 1) * rq]] * NSA_HPG, axis=0) > 0.5
        online_update(g, _dot_nt(_stack_heads(q_ref, g), kg) * ATT_SCALE, ok, vg)

    @pl.when(i == pl.num_programs(1) - 1)
    def _():
        n_past_blocks = past_len // SEL_BLOCK
        pad_ref[...] = jnp.zeros_like(pad_ref)
        pad_ref[0:rq, :] = ksn_ref[...]
        sel_new = _dot_nt(eye, selt_ref[n_past_blocks:n_past_blocks + LANE, :])
        kpos = past_len + lax.broadcasted_iota(jnp.int32, (rows, LANE), 1)
        causal = kpos <= t_of((rows, LANE))
        o_s = []
        for g in range(NSA_KV_HEADS):
            kn = pad_ref[:, g * NSA_DH:(g + 1) * NSA_DH].astype(BF16)
            vn = pad_ref[:, KV_W + g * NSA_DH:KV_W + (g + 1) * NSA_DH].astype(BF16)
            pk = jnp.concatenate([sel_new[g * rq:(g + 1) * rq, 0:1]] * NSA_HPG, axis=0)
            ok = jnp.where(causal, pk, 0.0) > 0.5
            online_update(g, _dot_nt(_stack_heads(q_ref, g), kn) * ATT_SCALE, ok, vn)
            o_s.append(acc_ref[g] / jnp.maximum(l_ref[g], 1e-30))

        pad_ref[0:rq, :] = kwn_ref[...]
        n_buf = win_ref.shape[0]
        kposw = past_len - n_buf + lax.broadcasted_iota(jnp.int32, (rows, n_buf + LANE), 1)
        tw = t_of((rows, n_buf + LANE))
        okw = jnp.where(kposw <= tw, jnp.where(kposw > tw - WINDOW, 1.0, 0.0), 0.0) > 0.5
        gt = jax.nn.sigmoid(gt_ref[...])
        for g in range(NSA_KV_HEADS):
            kw = jnp.concatenate([win_ref[:, 0, g, :], pad_ref[:, g * NSA_DH:(g + 1) * NSA_DH]],
                                 axis=0).astype(BF16)
            vw = jnp.concatenate([win_ref[:, 1, g, :], pad_ref[:, KV_W + g * NSA_DH:KV_W + (g + 1) * NSA_DH]],
                                 axis=0).astype(BF16)
            p_w = _masked_softmax(_dot_nt(_stack_heads(q_ref, g), kw) * ATT_SCALE, okw)
            o_w = _dot(p_w.astype(BF16), vw)
            for p in range(NSA_HPG):
                col = (g * NSA_HPG + p) * NSA_DH
                r = slice(p * rq, (p + 1) * rq)
                gc = g * LANE + p
                y = (gt[:, gc:gc + 1] * oc_ref[:, col:col + NSA_DH]
                     + gt[:, gc + NSA_HPG:gc + NSA_HPG + 1] * o_s[g][r]
                     + gt[:, gc + 2 * NSA_HPG:gc + 2 * NSA_HPG + 1] * o_w[r])
                o_ref[:, col:col + NSA_DH] = y


def _nsa_sample_attend(proj, o_c, sel_t, sel_pool, win_kv, layer, page_table, past_len):
    batch, n_pages = page_table.shape
    page_rows = sel_pool.shape[2]
    steps = n_pages // PAGES_PER_STEP
    keys = PAGES_PER_STEP * page_rows
    e = _block_expand_matrix(1, LANE, keys, keys // SEL_BLOCK)[0]
    n_buf = win_kv.shape[2]
    qw = NSA_HEADS * NSA_DH
    rq = SAMPLE_ROWS
    grid_spec = pltpu.PrefetchScalarGridSpec(
        num_scalar_prefetch=1, grid=(batch, steps),
        in_specs=_page_specs(layer, page_rows) + [
            pl.BlockSpec((rq, qw), lambda b, i, pt: (b, COL_NQ // qw)),
            pl.BlockSpec((None, SEL_ROWS_SAMPLE, LANE), lambda b, i, pt: (b, 0, 0)),
            pl.BlockSpec(e.shape, lambda b, i, pt: (0, 0)),
            pl.BlockSpec((rq, 2 * KV_W), lambda b, i, pt: (b, COL_KVS // (2 * KV_W))),
            pl.BlockSpec((rq, 2 * KV_W), lambda b, i, pt: (b, COL_KVW // (2 * KV_W))),
            pl.BlockSpec((None, None, n_buf, 2, NSA_KV_HEADS, NSA_DH), lambda b, i, pt: (layer, b, 0, 0, 0, 0)),
            pl.BlockSpec((rq, NSA_KV_HEADS * LANE), lambda b, i, pt: (b, COL_NG // (NSA_KV_HEADS * LANE))),
            pl.BlockSpec((rq, qw), lambda b, i, pt: (b, 0))],
        out_specs=pl.BlockSpec((rq, qw), lambda b, i, pt: (b, 0)),
        scratch_shapes=[pltpu.VMEM((NSA_KV_HEADS, NSA_HPG * rq, 1), F32),
                        pltpu.VMEM((NSA_KV_HEADS, NSA_HPG * rq, 1), F32),
                        pltpu.VMEM((NSA_KV_HEADS, NSA_HPG * rq, NSA_DH), F32),
                        pltpu.VMEM((LANE, 2 * KV_W), F32)])
    return pl.pallas_call(
        functools.partial(_nsa_sample_attend_kernel, past_len=past_len, page_rows=page_rows),
        grid_spec=grid_spec,
        out_shape=jax.ShapeDtypeStruct((batch * rq, qw), F32),
        compiler_params=_cparams("parallel", "arbitrary"),
    )(page_table, *([sel_pool] * PAGES_PER_STEP), proj, sel_t, e, proj, proj, win_kv, proj, o_c)


def _prep_w_in(w_in):
    main = w_in[:, :, :COL_NG].astype(BF16)
    ng = w_in[:, :, COL_NG:COL_NG + 3 * NSA_HEADS].reshape(DEPTH, D_MODEL, 3, NSA_KV_HEADS, NSA_HPG)
    ng = ng.transpose(0, 1, 3, 2, 4).reshape(DEPTH, D_MODEL, NSA_KV_HEADS, 3 * NSA_HPG)
    ng = jnp.pad(ng, [(0, 0), (0, 0), (0, 0), (0, LANE - 3 * NSA_HPG)])
    return main, ng.reshape(DEPTH, D_MODEL, NSA_KV_HEADS * LANE).astype(BF16)


def _kv_outputs_kernel(*refs, n_tiles, n_win):
    srcs, (oc_ref, os_ref, ow_ref) = refs[:-3], refs[-3:]
    layer = pl.program_id(0)
    i = pl.program_id(2)

    def relayout(src, dst):
        for c in range(2):
            for g in range(NSA_KV_HEADS):
                dst[:, c, g, :] = _kv_rows(src, 0, src.shape[0], c, g)

    for l in range(DEPTH):
        @pl.when(layer == l)
        def _(l=l):
            relayout(srcs[3 * l], oc_ref)
            relayout(srcs[3 * l + 1], os_ref)

            @pl.when(i >= n_tiles - n_win)
            def _():
                relayout(srcs[3 * l + 2], ow_ref)


def _kv_outputs(projs, batch, t, win_buf):
    tile = 512
    assert t % tile == 0 and win_buf % tile == 0
    n_tiles, n_win = t // tile, win_buf // tile
    w = 2 * KV_W
    src_spec = lambda col: pl.BlockSpec((tile, w), lambda l, b, i: (b * n_tiles + i, col // w))
    out_block = (None, None, tile, 2, NSA_KV_HEADS, NSA_DH)
    full = pl.BlockSpec(out_block, lambda l, b, i: (l, b, i, 0, 0, 0))
    last = pl.BlockSpec(out_block, lambda l, b, i: (l, b, jnp.maximum(i - (n_tiles - n_win), 0), 0, 0, 0))
    shape = lambda rows: jax.ShapeDtypeStruct((DEPTH, batch, rows, 2, NSA_KV_HEADS, NSA_DH), F32)
    return pl.pallas_call(
        functools.partial(_kv_outputs_kernel, n_tiles=n_tiles, n_win=n_win),
        grid=(DEPTH, batch, n_tiles),
        in_specs=[src_spec(col) for _ in range(DEPTH) for col in (COL_KVC, COL_KVS, COL_KVW)],
        out_specs=[full, full, last],
        out_shape=[shape(t), shape(t), shape(win_buf)],
        compiler_params=_cparams("parallel", "parallel", "arbitrary"),
    )(*[p for p in projs for _ in range(3)])


def _tile_mods(mod_rows, sub, tiles_per_seq=None, rows_per_seq=None):
    out = []
    for k in range(3):
        m = mod_rows[:, (sub * 3 + k) * D_MODEL:(sub * 3 + k + 1) * D_MODEL]
        if tiles_per_seq is not None:
            out.append(jnp.repeat(m, tiles_per_seq, axis=0)[:, None, :])
        else:
            out.append(jnp.repeat(m, rows_per_seq, axis=0)[None])
    return out


def kernel(x_prompt, x_sample, cache_cmp_kv, cache_sel_kv, cache_win_kv, state_ret, page_table, c_prompt, c_sample,
           norm_g, w_ada, b_ada, w_in, w_out, cmp_pe, cmp_w1, cmp_b1, cmp_w2, ffn_w_gate, ffn_w_up, ffn_w_down,
           final_g):
    bp, t, _ = x_prompt.shape
    bs, ts, _ = x_sample.shape
    win_buf = cache_win_kv.shape[2]
    n_pool, page_rows = cache_cmp_kv.shape[1:3]
    past_len = page_table.shape[1] * page_rows
    assert ts <= SAMPLE_ROWS and ts < CMP_STRIDE and past_len % CMP_STRIDE == 0
    assert page_table.shape[1] % PAGES_PER_STEP == 0 and past_len % SEL_BLOCK == 0
    n_sel_s = -(-(past_len + ts) // SEL_BLOCK)
    assert n_sel_s <= past_len // SEL_BLOCK + 1 and past_len // SEL_BLOCK + LANE <= SEL_ROWS_SAMPLE

    c_all = jnp.concatenate([c_prompt, c_sample], axis=0)
    c_all = jnp.pad(c_all, [(0, -c_all.shape[0] % 8), (0, 0)])
    mod = _modulation(c_all, w_ada, b_ada)

    bm = 512
    bm_p = 1024
    xp = x_prompt.reshape(bp * t, D_MODEL)
    pos_p = jnp.arange(t, dtype=jnp.int32)
    s0_p = jnp.zeros((bp, RET_HEADS, RET_DK, RET_DV), F32)
    rq = SAMPLE_ROWS
    rows_s = bs * rq
    xs = jnp.pad(x_sample, [(0, 0), (0, rq - ts), (0, 0)]).reshape(rows_s, D_MODEL)
    pos_s = past_len + jnp.arange(rq, dtype=jnp.int32)
    projs, pr, sc, ss, sw, sr = ([] for _ in range(6))
    wg = ffn_w_gate.astype(BF16)
    wu = ffn_w_up.astype(BF16)
    wd = ffn_w_down.astype(BF16)
    w_main, w_gate = _prep_w_in(w_in)
    w_out_b = w_out.astype(BF16)
    for l in range(DEPTH):
        cmp_w = _compress_weights(cmp_pe[l], cmp_w1[l], cmp_b1[l], cmp_w2[l])
        mod_p = mod[l, :bp]

        m0 = _tile_mods(mod_p, 0, tiles_per_seq=t // bm)
        xp = _ffn(xp, norm_g[l, 0], m0, wg, wu, wd, l, 0, bm)
        m1 = _tile_mods(mod_p, 1, tiles_per_seq=t // bm_p)
        proj = _proj(xp, norm_g[l, 1], m1, w_main, w_gate, l, bm_p)
        y_ret, s_ret = _retention(proj, s0_p, pos_p, bp, t, RET_CHUNK, RET_CHUNK, BF16)
        kc = _compress_prompt(proj, cmp_w, bp, t)
        y_nsa = _nsa_prompt(proj, kc, bp, t)
        xp = _outproj(xp, y_ret, y_nsa, m1[2], w_out_b, l, bm_p)
        m2 = _tile_mods(mod_p, 2, tiles_per_seq=t // bm)
        xp = _ffn(xp, norm_g[l, 2], m2, wg, wu, wd, l, 1, bm)
        projs.append(proj)
        pr.append(s_ret)

        mod_s = mod[l, bp:bp + bs]
        m0 = _tile_mods(mod_s, 0, rows_per_seq=rq)
        xs = _ffn(xs, norm_g[l, 0], m0, wg, wu, wd, l, 0, rows_s)
        m1 = _tile_mods(mod_s, 1, rows_per_seq=rq)
        proj_s = _proj(xs, norm_g[l, 1], m1, w_main, w_gate, l, rows_s)
        y_ret, s_ret = _retention(proj_s, state_ret[l], pos_s, bs, rq, ts, rq, F32)
        kc_s = _compress_sample(cache_cmp_kv, l, page_table, cmp_w)
        o_c, sel_t = _nsa_sample_select(proj_s, kc_s, bs, past_len, n_sel_s)
        y_nsa = _nsa_sample_attend(proj_s, o_c, sel_t, cache_sel_kv, cache_win_kv, l, page_table, past_len)
        xs = _outproj(xs, y_ret, y_nsa, m1[2], w_out_b, l, rows_s)
        m2 = _tile_mods(mod_s, 2, rows_per_seq=rq)
        xs = _ffn(xs, norm_g[l, 2], m2, wg, wu, wd, l, 1, rows_s)

        new_kv = lambda c0: proj_s[:, c0:c0 + 2 * KV_W].reshape(bs, rq, 2, NSA_KV_HEADS, NSA_DH)[:, :ts]
        sc.append(new_kv(COL_KVC))
        ss.append(new_kv(COL_KVS))
        sw.append(jnp.concatenate([cache_win_kv[l], new_kv(COL_KVW)], axis=1)[:, ts:])
        sr.append(s_ret)

    pc, ps, pw = _kv_outputs(projs, bp, t, win_buf)
    y_prompt = _final_norm(xp, final_g, bm).reshape(bp, t, D_MODEL)
    y_sample = _final_norm(xs, final_g, rows_s).reshape(bs, rq, D_MODEL)[:, :ts]
    return (y_prompt, y_sample, pc, ps, pw, jnp.stack(pr),
            jnp.stack(sc), jnp.stack(ss), jnp.stack(sw), jnp.stack(sr))
```

```python
import functools

import jax
import jax.numpy as jnp
import numpy as np
from jax import lax
from jax.experimental import pallas as pl
from jax.experimental.pallas import tpu as pltpu

F32 = jnp.float32
BF16 = jnp.bfloat16

D_MODEL = 4096
DEPTH = 2
RET_HEADS = 8
RET_DK = 128
RET_DV = 256
RET_CHUNK = 128
ROPE_BASE = 10000.0
NSA_HEADS = 16
NSA_DH = 128
NSA_KV_HEADS = 4
NSA_HPG = NSA_HEADS // NSA_KV_HEADS
CMP_BLOCK = 32
CMP_STRIDE = 16
CMP_RATIO = CMP_BLOCK // CMP_STRIDE
CMP_HIDDEN = 2 * NSA_DH
SEL_BLOCK = 64
SEL_TOPK = 16
WINDOW = 512
N_SUB = 3
RMS_EPS = 1e-6
NEG_INF = -1e30
FORCE_SCORE = 1e9
KV_W = NSA_KV_HEADS * NSA_DH
ATT_SCALE = NSA_DH ** -0.5

COL_RQ = 0
COL_RK = COL_RQ + RET_HEADS * RET_DK
COL_RV = COL_RK + RET_HEADS * RET_DK
COL_RG = COL_RV + RET_HEADS * RET_DV
COL_NQ = COL_RG + RET_HEADS * RET_DV
COL_KVC = COL_NQ + NSA_HEADS * NSA_DH
COL_KVS = COL_KVC + 2 * KV_W
COL_KVW = COL_KVS + 2 * KV_W
COL_NG = COL_KVW + 2 * KV_W
N_PROJ = COL_NG + NSA_KV_HEADS * 128

LANE = 128
SAMPLE_ROWS = 8
VMEM_LIMIT_BYTES = 60 * 1024 * 1024


def _cparams(*sem):
    return pltpu.CompilerParams(dimension_semantics=sem, vmem_limit_bytes=VMEM_LIMIT_BYTES)


def _silu(x):
    return x * jax.nn.sigmoid(x)


def _dot(a, b):
    return jnp.dot(a, b, preferred_element_type=F32)


def _dot_nt(a, b):
    return lax.dot_general(a, b, (((1,), (1,)), ((), ())), preferred_element_type=F32)


def _masked_softmax(s, valid):
    s = jnp.where(valid, s, NEG_INF)
    m = jnp.max(s, axis=-1, keepdims=True)
    e = jnp.where(valid, jnp.exp(s - m), 0.0)
    return e / jnp.maximum(jnp.sum(e, axis=-1, keepdims=True), 1e-30)


def _mod_kernel(c_ref, w_ref, b_ref, o_ref):
    a = _silu(c_ref[...]).astype(BF16)
    o_ref[...] = _dot(a, w_ref[...].astype(BF16)) + b_ref[...]


def _modulation(c_all, w_ada, b_ada):
    rows = c_all.shape[0]
    n = w_ada.shape[-1]
    bn = 512
    return pl.pallas_call(
        _mod_kernel,
        grid=(DEPTH, n // bn),
        in_specs=[pl.BlockSpec((rows, D_MODEL), lambda l, j: (0, 0)),
                  pl.BlockSpec((None, D_MODEL, bn), lambda l, j: (l, 0, j)),
                  pl.BlockSpec((None, 1, bn), lambda l, j: (l, 0, j))],
        out_specs=pl.BlockSpec((None, rows, bn), lambda l, j: (l, 0, j)),
        out_shape=jax.ShapeDtypeStruct((DEPTH, rows, n), F32),
        compiler_params=_cparams("parallel", "parallel"),
    )(c_all, w_ada, b_ada.reshape(DEPTH, 1, n))


def _norm_modulate_into(h_ref, x_ref, g_ref, sh_ref, sc_ref):
    rows = x_ref.shape[0]
    per_row = sh_ref.shape[0] != 1
    step = min(rows, 128)
    for r0 in range(0, rows, step):
        x = x_ref[r0:r0 + step, :]
        y = x * lax.rsqrt(jnp.mean(x * x, axis=-1, keepdims=True) + RMS_EPS) * g_ref[...]
        sc = sc_ref[r0:r0 + step, :] if per_row else sc_ref[...]
        sh = sh_ref[r0:r0 + step, :] if per_row else sh_ref[...]
        h_ref[r0:r0 + step, :] = (y * (1.0 + sc) + sh).astype(h_ref.dtype)


def _mod_spec(mod_rows, bm):
    return pl.BlockSpec((None, mod_rows, D_MODEL), lambda i, j: (i, 0, 0))


def _ffn_kernel(x_ref, g_ref, sh_ref, sc_ref, gt_ref, wg_ref, wu_ref, wd_ref, o_ref, *rest, n_chunk):
    j = pl.program_id(1)
    h_ref = rest[-1]
    if len(rest) > 1:
        for src, dst in zip((wg_ref, wu_ref, wd_ref), rest[:3]):
            dst[...] = src[...].astype(BF16)
        wg_ref, wu_ref, wd_ref = rest[:3]

    @pl.when(j == 0)
    def _():
        _norm_modulate_into(h_ref, x_ref, g_ref, sh_ref, sc_ref)
        o_ref[...] = jnp.zeros_like(o_ref)

    h = h_ref[...]
    a = (_silu(_dot(h, wg_ref[...])) * _dot(h, wu_ref[...])).astype(BF16)
    for n0 in range(0, D_MODEL, n_chunk):
        o_ref[:, n0:n0 + n_chunk] += _dot(a, wd_ref[:, n0:n0 + n_chunk])

    @pl.when(j == pl.num_programs(1) - 1)
    def _():
        rows = x_ref.shape[0]
        per_row = gt_ref.shape[0] != 1
        step = min(rows, 128)
        for r0 in range(0, rows, step):
            gt = gt_ref[r0:r0 + step, :] if per_row else gt_ref[...]
            o_ref[r0:r0 + step, :] = x_ref[r0:r0 + step, :] + (0.5 * gt) * o_ref[r0:r0 + step, :]


def _ffn(x, g, mods, weights, bm, layer_half=None):
    rows = x.shape[0]
    wg, wu, wd = weights
    d_ff = wg.shape[-1]
    tf = 256
    shift, scale, gate = mods
    mr = shift.shape[1]
    up_block, down_block = (D_MODEL, tf), (tf, D_MODEL)
    up_idx, down_idx = (lambda i, j: (0, j)), (lambda i, j: (j, 0))
    out_specs = [pl.BlockSpec((bm, D_MODEL), lambda i, j: (i, 0))]
    out_shape = [jax.ShapeDtypeStruct((rows, D_MODEL), F32)]
    if layer_half is None:
        w_specs = [pl.BlockSpec(up_block, up_idx), pl.BlockSpec(up_block, up_idx), pl.BlockSpec(down_block, down_idx)]
    else:
        assert rows == bm
        l, k = layer_half
        w_specs = [pl.BlockSpec((None, None) + up_block, lambda i, j: (l, k, 0, j)),
                   pl.BlockSpec((None, None) + up_block, lambda i, j: (l, k, 0, j)),
                   pl.BlockSpec((None, None) + down_block, lambda i, j: (l, k, j, 0))]
        out_specs += [pl.BlockSpec(up_block, up_idx), pl.BlockSpec(up_block, up_idx),
                      pl.BlockSpec(down_block, down_idx)]
        out_shape += [jax.ShapeDtypeStruct((D_MODEL, d_ff), BF16)] * 2 + [jax.ShapeDtypeStruct((d_ff, D_MODEL), BF16)]
    out = pl.pallas_call(
        functools.partial(_ffn_kernel, n_chunk=512),
        grid=(rows // bm, d_ff // tf),
        in_specs=[pl.BlockSpec((bm, D_MODEL), lambda i, j: (i, 0)),
                  pl.BlockSpec((1, D_MODEL), lambda i, j: (0, 0)),
                  _mod_spec(mr, bm), _mod_spec(mr, bm), _mod_spec(mr, bm)] + w_specs,
        out_specs=out_specs,
        out_shape=out_shape,
        scratch_shapes=[pltpu.VMEM((bm, D_MODEL), BF16)],
        compiler_params=_cparams("parallel", "arbitrary"),
    )(x, g.reshape(1, D_MODEL), shift, scale, gate, wg, wu, wd)
    return out[0] if layer_half is None else (out[0], tuple(out[1:]))


def _proj_kernel(x_ref, g_ref, sh_ref, sc_ref, w_ref, wgate_ref, o_ref, *rest, n_main):
    j = pl.program_id(1)
    h_ref = rest[-1]

    @pl.when(j == 0)
    def _():
        _norm_modulate_into(h_ref, x_ref, g_ref, sh_ref, sc_ref)

    @pl.when(j < n_main)
    def _():
        if len(rest) > 1:
            rest[0][...] = w_ref[...].astype(BF16)
            o_ref[...] = _dot(h_ref[...], rest[0][...])
        else:
            o_ref[...] = _dot(h_ref[...], w_ref[...])

    @pl.when(j >= n_main)
    def _():
        o_ref[...] = _dot(h_ref[...], wgate_ref[...])


def _proj(x, g, mods, w_main, w_gate, layer, bm, cast=False):
    rows = x.shape[0]
    bn = N_PROJ - COL_NG
    n_main = COL_NG // bn
    shift, scale, _ = mods
    mr = shift.shape[1]
    main_idx = lambda i, j: (0, jnp.minimum(j, n_main - 1))
    out_specs = [pl.BlockSpec((bm, bn), lambda i, j: (i, j))]
    out_shape = [jax.ShapeDtypeStruct((rows, N_PROJ), F32)]
    if cast:
        assert rows == bm
        w_spec = pl.BlockSpec((None, D_MODEL, bn), lambda i, j: (layer, 0, jnp.minimum(j, n_main - 1)))
        out_specs.append(pl.BlockSpec((D_MODEL, bn), main_idx))
        out_shape.append(jax.ShapeDtypeStruct((D_MODEL, COL_NG), BF16))
    else:
        w_spec = pl.BlockSpec((D_MODEL, bn), main_idx)
    out = pl.pallas_call(
        functools.partial(_proj_kernel, n_main=n_main),
        grid=(rows // bm, n_main + 1),
        in_specs=[pl.BlockSpec((bm, D_MODEL), lambda i, j: (i, 0)),
                  pl.BlockSpec((1, D_MODEL), lambda i, j: (0, 0)),
                  _mod_spec(mr, bm), _mod_spec(mr, bm),
                  w_spec,
                  pl.BlockSpec((None, D_MODEL, bn), lambda i, j: (layer, 0, 0))],
        out_specs=out_specs,
        out_shape=out_shape,
        scratch_shapes=[pltpu.VMEM((bm, D_MODEL), BF16)],
        compiler_params=_cparams("parallel", "arbitrary"),
    )(x, g.reshape(1, D_MODEL), shift, scale, w_main, w_gate)
    return tuple(out) if cast else out[0]


def _outproj_kernel(x_ref, yr_ref, yn_ref, gt_ref, wa_ref, wb_ref, o_ref, *cast_out):
    if cast_out:
        for src, dst in zip((wa_ref, wb_ref), cast_out):
            dst[...] = src[...].astype(BF16)
        wa_ref, wb_ref = cast_out
    y = _dot(yr_ref[...].astype(BF16), wa_ref[...]) + _dot(yn_ref[...].astype(BF16), wb_ref[...])
    o_ref[...] = x_ref[...] + gt_ref[...] * y


def _outproj(x, y_ret, y_nsa, gate, w_halves, bm, layer=None):
    rows = x.shape[0]
    half = y_ret.shape[1]
    bn = 512
    mr = gate.shape[1]
    out_specs = [pl.BlockSpec((bm, bn), lambda i, j: (i, j))]
    out_shape = [jax.ShapeDtypeStruct((rows, D_MODEL), F32)]
    if layer is None:
        w_specs = [pl.BlockSpec((half, bn), lambda i, j: (0, j))] * 2
    else:
        assert rows == bm
        w_specs = [pl.BlockSpec((None, half, bn), lambda i, j: (layer, 0, j)),
                   pl.BlockSpec((None, half, bn), lambda i, j: (layer, 1, j))]
        out_specs += [pl.BlockSpec((half, bn), lambda i, j: (0, j))] * 2
        out_shape += [jax.ShapeDtypeStruct((half, D_MODEL), BF16)] * 2
    out = pl.pallas_call(
        _outproj_kernel,
        grid=(rows // bm, D_MODEL // bn),
        in_specs=[pl.BlockSpec((bm, bn), lambda i, j: (i, j)),
                  pl.BlockSpec((bm, half), lambda i, j: (i, 0)),
                  pl.BlockSpec((bm, half), lambda i, j: (i, 0)),
                  pl.BlockSpec((None, mr, bn), lambda i, j: (i, 0, j))] + w_specs,
        out_specs=out_specs,
        out_shape=out_shape,
        compiler_params=_cparams("parallel", "arbitrary"),
    )(x, y_ret, y_nsa, gate, *w_halves)
    return out[0] if layer is None else (out[0], tuple(out[1:]))


def _final_norm_kernel(x_ref, g_ref, o_ref):
    x = x_ref[...]
    o_ref[...] = x * lax.rsqrt(jnp.mean(x * x, axis=-1, keepdims=True) + RMS_EPS) * g_ref[...]


def _final_norm(x, g, bm):
    rows = x.shape[0]
    return pl.pallas_call(
        _final_norm_kernel,
        grid=(rows // bm,),
        in_specs=[pl.BlockSpec((bm, D_MODEL), lambda i: (i, 0)),
                  pl.BlockSpec((1, D_MODEL), lambda i: (0, 0))],
        out_specs=pl.BlockSpec((bm, D_MODEL), lambda i: (i, 0)),
        out_shape=jax.ShapeDtypeStruct((rows, D_MODEL), F32),
        compiler_params=_cparams("parallel"),
    )(x, g.reshape(1, D_MODEL))


def _retention_kernel(q_ref, k_ref, v_ref, rg_ref, cos_ref, sin_ref, dm_ref, qd_ref, kd_ref, cd_ref, s0_ref,
                      y_ref, s_ref, *pad, cq):
    @pl.when(pl.program_id(1) == 0)
    def _():
        s_ref[...] = s0_ref[...]

    cos = cos_ref[...]
    sin = sin_ref[...]
    padded = cq != RET_CHUNK
    if padded:
        kp_ref, vp_ref = pad
        kp_ref[...] = jnp.zeros_like(kp_ref)
        vp_ref[...] = jnp.zeros_like(vp_ref)
    for h in range(RET_HEADS):
        q = q_ref[:, h * RET_DK:(h + 1) * RET_DK]
        k = k_ref[:, h * RET_DK:(h + 1) * RET_DK]
        q = q * cos + pltpu.roll(q, RET_DK // 2, 1) * sin
        k = (k * cos + pltpu.roll(k, RET_DK // 2, 1) * sin) * (RET_DK ** -0.5)
        v = v_ref[:, h * RET_DV:(h + 1) * RET_DV]
        if padded:
            kp_ref[0:cq, :] = k
            vp_ref[0:cq, :] = v
            k = kp_ref[...]
            v = vp_ref[...]
        qb = q.astype(BF16)
        vb = v.astype(BF16)
        att = _dot_nt(qb, k.astype(BF16)) * dm_ref[h]
        s = s_ref[h]
        o = _dot(att.astype(BF16), vb) + _dot(qb, s.astype(BF16)) * qd_ref[h]
        kd = (k * kd_ref[h]).T.astype(BF16)
        s_ref[h] = s * cd_ref[h] + _dot(kd, vb)
        o = o * lax.rsqrt(jnp.mean(o * o, axis=-1, keepdims=True) + RMS_EPS)
        y = _silu(rg_ref[:, h * RET_DV:(h + 1) * RET_DV]) * o
        y_ref[:, h * RET_DV:(h + 1) * RET_DV] = y.astype(y_ref.dtype)


def _retention_tables(pos, chunk, cq):
    half = RET_DK // 2
    inv = jnp.power(ROPE_BASE, -jnp.arange(half, dtype=F32) * 2.0 / RET_DK)
    ang = pos.astype(F32)[:, None] * inv[None, :]
    cos = jnp.cos(ang)
    sin = jnp.sin(ang)
    cos2 = jnp.concatenate([cos, cos], axis=-1)
    sin2 = jnp.concatenate([-sin, sin], axis=-1)
    log_g = jnp.log(1.0 - jnp.exp2(-5.0 - jnp.arange(RET_HEADS, dtype=F32)))
    i = jnp.arange(chunk, dtype=F32)
    diff = i[:, None] - i[None, :]
    dmask = jnp.where(diff >= 0, jnp.exp(log_g[:, None, None] * jnp.maximum(diff, 0.0)), 0.0)
    q_dec = jnp.exp(log_g[None, :] * (i[:, None] + 1.0))
    k_dec = jnp.exp(log_g[None, :] * (chunk - 1.0 - i[:, None]))
    c_dec = jnp.exp(log_g * chunk)
    dm = jnp.zeros((RET_HEADS, cq, RET_CHUNK), F32).at[:, :chunk, :chunk].set(dmask)
    qd = jnp.zeros((RET_HEADS, cq, RET_DV), F32).at[:, :chunk, :].set(
        jnp.broadcast_to(q_dec.T[:, :, None], (RET_HEADS, chunk, RET_DV)))
    kd = jnp.zeros((RET_HEADS, RET_CHUNK, RET_DK), F32).at[:, :chunk, :].set(
        jnp.broadcast_to(k_dec.T[:, :, None], (RET_HEADS, chunk, RET_DK)))
    cd = jnp.broadcast_to(c_dec[:, None, None], (RET_HEADS, 1, RET_DV))
    return cos2, sin2, dm, qd, kd, cd


def _retention(proj, s0, pos, batch, t_rows, chunk, cq, out_dtype):
    nc = t_rows // cq
    cos2, sin2, dm, qd, kd, cd = _retention_tables(pos, chunk, cq)
    hk = RET_HEADS * RET_DK
    hv = RET_HEADS * RET_DV
    row = lambda b, c: b * nc + c
    scratch = [] if cq == RET_CHUNK else [pltpu.VMEM((RET_CHUNK, RET_DK), F32), pltpu.VMEM((RET_CHUNK, RET_DV), F32)]
    full3 = lambda shape: pl.BlockSpec(shape, lambda b, c: (0, 0, 0))
    return pl.pallas_call(
        functools.partial(_retention_kernel, cq=cq),
        grid=(batch, nc),
        in_specs=[pl.BlockSpec((cq, hk), lambda b, c: (row(b, c), COL_RQ // hk)),
                  pl.BlockSpec((cq, hk), lambda b, c: (row(b, c), COL_RK // hk)),
                  pl.BlockSpec((cq, hv), lambda b, c: (row(b, c), COL_RV // hv)),
                  pl.BlockSpec((cq, hv), lambda b, c: (row(b, c), COL_RG // hv)),
                  pl.BlockSpec((cq, RET_DK), lambda b, c: (c, 0)),
                  pl.BlockSpec((cq, RET_DK), lambda b, c: (c, 0)),
                  full3(dm.shape), full3(qd.shape), full3(kd.shape), full3(cd.shape),
                  pl.BlockSpec((None, RET_HEADS, RET_DK, RET_DV), lambda b, c: (b, 0, 0, 0))],
        out_specs=[pl.BlockSpec((cq, hv), lambda b, c: (row(b, c), 0)),
                   pl.BlockSpec((None, RET_HEADS, RET_DK, RET_DV), lambda b, c: (b, 0, 0, 0))],
        out_shape=[jax.ShapeDtypeStruct((batch * t_rows, hv), out_dtype),
                   jax.ShapeDtypeStruct((batch, RET_HEADS, RET_DK, RET_DV), F32)],
        scratch_shapes=scratch,
        compiler_params=_cparams("parallel", "arbitrary"),
    )(proj, proj, proj, proj, cos2, sin2, dm, qd, kd, cd, s0)


KV_PER_ROW = 2 * NSA_KV_HEADS


def _kv_rows(ref, r0, rows, c, g, step=1):
    if ref.shape[1] == NSA_DH:
        return ref[pl.ds(r0 * KV_PER_ROW + c * NSA_KV_HEADS + g, rows, stride=step * KV_PER_ROW), :]
    assert step == 1
    col = (c * NSA_KV_HEADS + g) * NSA_DH
    return ref[r0:r0 + rows, col:col + NSA_DH]


def _compress_kernel(*refs, n_src):
    srcs = refs[:n_src]
    perm_ref, w1_ref, pe_ref, b1_ref, w2_ref, o_ref, x_ref, prev_ref = refs[n_src:]
    cache_view = srcs[0].shape[1] == NSA_DH
    src_tokens = srcs[0].shape[0] // (KV_PER_ROW if cache_view else 1)
    s_src = src_tokens // CMP_STRIDE
    s_tot = n_src * s_src
    rows = NSA_KV_HEADS * s_tot

    @pl.when(pl.program_id(1) == 0)
    def _():
        prev_ref[...] = jnp.zeros_like(prev_ref)

    for si, src in enumerate(srcs):
        for c in range(2):
            for g in range(NSA_KV_HEADS):
                r0 = g * s_tot + si * s_src
                if cache_view:
                    for s in range(CMP_STRIDE):
                        x_ref[c, r0:r0 + s_src, s * NSA_DH:(s + 1) * NSA_DH] = _kv_rows(src, s, s_src, c, g, CMP_STRIDE)
                    continue
                grp = perm_ref.shape[0]
                segs = grp // CMP_STRIDE
                for gi in range(src_tokens // grp):
                    blk = _dot(perm_ref[...], _kv_rows(src, gi * grp, grp, c, g).astype(BF16))
                    for s in range(CMP_STRIDE):
                        x_ref[c, r0 + gi * segs:r0 + (gi + 1) * segs, s * NSA_DH:(s + 1) * NSA_DH] = (
                            blk[s * segs:(s + 1) * segs, :])

    first = (lax.broadcasted_iota(jnp.int32, (rows, CMP_HIDDEN), 0) & (s_tot - 1)) == 0
    for c in range(2):
        w1 = w1_ref[c]
        part = _dot(x_ref[c].astype(BF16), w1)
        pe_part = _dot(pe_ref[c], w1)
        bias = b1_ref[c] + pe_part[0:1, :CMP_HIDDEN] + pe_part[1:2, CMP_HIDDEN:]
        p0 = part[:, :CMP_HIDDEN]
        p1 = part[:, CMP_HIDDEN:]
        shifted = jnp.where(first, pltpu.roll(prev_ref[c], rows - s_tot + 1, 0), pltpu.roll(p0, 1, 0))
        prev_ref[c] = p0
        kc = _dot(_silu(bias + shifted + p1).astype(BF16), w2_ref[c])
        for g in range(NSA_KV_HEADS):
            col = (c * NSA_KV_HEADS + g) * NSA_DH
            o_ref[:, col:col + NSA_DH] = kc[g * s_tot:(g + 1) * s_tot].astype(o_ref.dtype)


def _compress_weights(cmp_pe, cmp_w1, cmp_b1, cmp_w2):
    w1 = cmp_w1.reshape(2, CMP_RATIO, CMP_STRIDE, NSA_DH, CMP_HIDDEN).transpose(0, 2, 3, 1, 4)
    w1 = w1.reshape(2, CMP_STRIDE * NSA_DH, CMP_RATIO * CMP_HIDDEN).astype(BF16)
    pe = cmp_pe.reshape(2, CMP_RATIO, CMP_STRIDE * NSA_DH)
    pe = jnp.pad(pe, [(0, 0), (0, 8 - CMP_RATIO), (0, 0)]).astype(BF16)
    return _segment_permutation(), w1, pe, cmp_b1.reshape(2, 1, CMP_HIDDEN), cmp_w2.astype(BF16)


COMPRESS_GROUP_ROWS = 128


def _segment_permutation():
    segs = COMPRESS_GROUP_ROWS // CMP_STRIDE
    r = np.arange(COMPRESS_GROUP_ROWS)
    src = (r % segs) * CMP_STRIDE + r // segs
    return jnp.asarray(src[:, None] == np.arange(COMPRESS_GROUP_ROWS)[None, :], BF16)


def _compress_specs(idx):
    kdim = CMP_STRIDE * NSA_DH
    return [pl.BlockSpec((COMPRESS_GROUP_ROWS, COMPRESS_GROUP_ROWS), idx(lambda *_: (0, 0))),
            pl.BlockSpec((2, kdim, CMP_RATIO * CMP_HIDDEN), idx(lambda *_: (0, 0, 0))),
            pl.BlockSpec((2, 8, kdim), idx(lambda *_: (0, 0, 0))),
            pl.BlockSpec((2, 1, CMP_HIDDEN), idx(lambda *_: (0, 0, 0))),
            pl.BlockSpec((2, CMP_HIDDEN, NSA_DH), idx(lambda *_: (0, 0, 0)))]


def _compress_scratch(s_tot):
    rows = NSA_KV_HEADS * s_tot
    return [pltpu.VMEM((2, rows, CMP_STRIDE * NSA_DH), F32), pltpu.VMEM((2, rows, CMP_HIDDEN), F32)]


def _compress_prompt(proj, cmp_w, batch, t):
    segs = t // CMP_STRIDE
    return pl.pallas_call(
        functools.partial(_compress_kernel, n_src=1),
        grid=(batch, 1),
        in_specs=[pl.BlockSpec((t, 2 * KV_W), lambda b, i: (b, COL_KVC // (2 * KV_W)))]
        + _compress_specs(lambda f: f),
        out_specs=pl.BlockSpec((None, segs, 2 * KV_W), lambda b, i: (b, 0, 0)),
        out_shape=jax.ShapeDtypeStruct((batch, segs, 2 * KV_W), BF16),
        scratch_shapes=_compress_scratch(segs),
        compiler_params=_cparams("parallel", "arbitrary"),
    )(proj, *cmp_w)


def _split_bf16(x):
    hi = x.astype(BF16)
    return hi, (x - hi.astype(F32)).astype(BF16)


def _selection_scores(psl_t, t_pos):
    j = lax.broadcasted_iota(jnp.int32, psl_t.shape, 0)
    jc = t_pos // SEL_BLOCK
    forced = (j == 0) | (j == jc) | (j == jc - 1)
    return jnp.where(j > jc, NEG_INF, jnp.where(forced, FORCE_SCORE, psl_t))


def _beats(row, jp, score, j):
    return jnp.where(row > score, 1.0, jnp.where(row == score, jnp.where(j > jp, 1.0, 0.0), 0.0))


def _nsa_prompt_kernel(q_ref, gt_ref, kck_ref, kcv_ref, ks_ref, vs_ref, kw_ref, vw_ref, at_ref, e_ref, cb_ref, wb_ref,
                       o_ref, ksb, vsb, kwb, vwb, m_ref, l_ref, acc_ref, *, tq, tk, n_sel):
    qi = pl.program_id(2)
    rows = NSA_HPG * tq

    @pl.when(qi == 0)
    def _():
        ksb[...] = ks_ref[...].astype(BF16)
        vsb[...] = vs_ref[...].astype(BF16)
        kwb[...] = kw_ref[...].astype(BF16)
        vwb[...] = vw_ref[...].astype(BF16)

    def t_of(shape):
        return qi * tq + (lax.broadcasted_iota(jnp.int32, shape, 0) & (tq - 1))

    q4 = (jnp.concatenate([q_ref[:, p * NSA_DH:(p + 1) * NSA_DH] for p in range(NSA_HPG)], axis=0)
          * ATT_SCALE).astype(BF16)

    n_slot = kck_ref.shape[0]
    slot = lax.broadcasted_iota(jnp.int32, (rows, n_slot), 1)
    valid = (slot >= 1) & (slot * CMP_STRIDE + (CMP_BLOCK - CMP_STRIDE - 1) <= t_of((rows, n_slot)))
    p_c = _masked_softmax(_dot_nt(q4, kck_ref[...]), valid)
    o_c = _dot(p_c.astype(BF16), kcv_ref[...])

    pg = p_c[0:tq]
    for p in range(1, NSA_HPG):
        pg = pg + p_c[p * tq:(p + 1) * tq]
    hi, lo = _split_bf16(pg)
    psl_t = _dot_nt(at_ref[...], hi) + _dot_nt(at_ref[...], lo)
    jshape = psl_t.shape
    j = lax.broadcasted_iota(jnp.int32, jshape, 0)
    score = _selection_scores(psl_t, qi * tq + lax.broadcasted_iota(jnp.int32, jshape, 1))
    rank = jnp.zeros(jshape, F32)
    for jp in range(n_sel):
        rank = rank + _beats(score[jp:jp + 1, :], jp, score, j)
    bias_t = jnp.where((rank < min(SEL_TOPK, n_sel)) & (j < n_sel), 0.0, NEG_INF)
    bias_t = jnp.concatenate([bias_t, jnp.full((e_ref.shape[1] - jshape[0], tq), NEG_INF, F32)], axis=0).astype(BF16)
    eye = jnp.where(lax.broadcasted_iota(jnp.int32, (tq, tq), 0) == lax.broadcasted_iota(jnp.int32, (tq, tq), 1),
                    1.0, 0.0).astype(BF16)
    sel_bias = _dot_nt(eye, bias_t).astype(BF16)

    m_ref[...] = jnp.full(m_ref.shape, NEG_INF, F32)
    l_ref[...] = jnp.zeros(l_ref.shape, F32)
    acc_ref[...] = jnp.zeros(acc_ref.shape, F32)

    def chunk(c, causal_bias):
        k0 = pl.multiple_of(c * tk, tk)
        bias = _dot(sel_bias, e_ref[c])
        if causal_bias is not None:
            bias = bias + causal_bias
        s = _dot_nt(q4, ksb[pl.ds(k0, tk), :]) + jnp.concatenate([bias] * NSA_HPG, axis=0)
        m_old = m_ref[...]
        m_new = jnp.maximum(m_old, jnp.max(s, axis=-1, keepdims=True))
        alpha = jnp.exp(m_old - m_new)
        e = jnp.exp(s - m_new)
        l_ref[...] = l_ref[...] * alpha + jnp.sum(e, axis=-1, keepdims=True)
        acc_ref[...] = acc_ref[...] * alpha + _dot(e.astype(BF16), vsb[pl.ds(k0, tk), :])
        m_ref[...] = m_new

    n_full = (qi * tq) // tk

    def full_chunk(c, carry):
        chunk(c, None)
        return carry

    lax.fori_loop(0, n_full, full_chunk, 0)
    chunk(n_full, cb_ref[qi % (tk // tq)])
    o_s = acc_ref[...] / jnp.maximum(l_ref[...], 1e-30)

    n_back = WINDOW // tq
    w_len = wb_ref.shape[2]
    qw_i = jnp.minimum(qi, n_back)
    w0 = pl.multiple_of((qi - qw_i) * tq, tq)
    s_w = _dot_nt(q4, kwb[pl.ds(w0, w_len), :]) + jnp.concatenate([wb_ref[qw_i]] * NSA_HPG, axis=0)
    e_w = jnp.exp(s_w - jnp.max(s_w, axis=-1, keepdims=True))
    p_w = e_w / jnp.maximum(jnp.sum(e_w, axis=-1, keepdims=True), 1e-30)
    o_w = _dot(p_w.astype(BF16), vwb[pl.ds(w0, w_len), :])

    gt = jax.nn.sigmoid(gt_ref[...])
    for p in range(NSA_HPG):
        r = slice(p * tq, (p + 1) * tq)
        y = (gt[:, p:p + 1] * o_c[r] + gt[:, NSA_HPG + p:NSA_HPG + p + 1] * o_s[r]
             + gt[:, 2 * NSA_HPG + p:2 * NSA_HPG + p + 1] * o_w[r])
        o_ref[:, p * NSA_DH:(p + 1) * NSA_DH] = y.astype(o_ref.dtype)


def _selection_sum_matrix(n_rows, n_slot, n_sel):
    ratio = SEL_BLOCK // CMP_STRIDE
    j = np.arange(n_rows)[:, None]
    m = np.arange(n_slot)[None, :]
    a = (j < n_sel) & (m >= ratio * j) & (m <= ratio * j + ratio + CMP_RATIO - 2)
    return jnp.asarray(a, BF16)


def _block_expand_matrix(n_chunks, n_rows, tk, blocks_per_chunk):
    c = np.arange(n_chunks)[:, None, None]
    j = np.arange(n_rows)[None, :, None]
    k = np.arange(tk)[None, None, :]
    return jnp.asarray((c * blocks_per_chunk + k // SEL_BLOCK) == j, BF16)


def _causal_bias_table(tq, tk):
    d = np.arange(tk // tq)[:, None, None]
    r = np.arange(tq)[None, :, None]
    k = np.arange(tk)[None, None, :]
    return jnp.asarray(np.where(k <= d * tq + r, 0.0, NEG_INF), F32)


def _window_bias_table(tq, n_back, w_len):
    d = np.arange(n_back + 1)[:, None, None]
    r = np.arange(tq)[None, :, None]
    k = np.arange(w_len)[None, None, :]
    rel = k - d * tq - r
    return jnp.asarray(np.where((rel <= 0) & (rel > -WINDOW), 0.0, NEG_INF), F32)


def _nsa_prompt(proj, kc, batch, t):
    tq, tk = 128, 512
    nq = t // tq
    n_sel = t // SEL_BLOCK
    n_slot = kc.shape[1]
    n_back = WINDOW // tq
    a_t = _selection_sum_matrix(-(-n_sel // 8) * 8, n_slot, n_sel)
    e = _block_expand_matrix(t // tk, LANE, tk, tk // SEL_BLOCK)
    cb = _causal_bias_table(tq, tk)
    wb = _window_bias_table(tq, n_back, min((n_back + 1) * tq, t))
    qw = NSA_HPG * NSA_DH
    kv = lambda base, c: pl.BlockSpec((t, NSA_DH), lambda b, g, qi: (b, (base + c * KV_W) // NSA_DH + g))
    return pl.pallas_call(
        functools.partial(_nsa_prompt_kernel, tq=tq, tk=tk, n_sel=n_sel),
        grid=(batch, NSA_KV_HEADS, nq),
        in_specs=[pl.BlockSpec((tq, qw), lambda b, g, qi: (b * nq + qi, COL_NQ // qw + g)),
                  pl.BlockSpec((tq, LANE), lambda b, g, qi: (b * nq + qi, COL_NG // LANE + g)),
                  pl.BlockSpec((None, n_slot, NSA_DH), lambda b, g, qi: (b, 0, g)),
                  pl.BlockSpec((None, n_slot, NSA_DH), lambda b, g, qi: (b, 0, NSA_KV_HEADS + g)),
                  kv(COL_KVS, 0), kv(COL_KVS, 1), kv(COL_KVW, 0), kv(COL_KVW, 1),
                  pl.BlockSpec(a_t.shape, lambda b, g, qi: (0, 0)),
                  pl.BlockSpec(e.shape, lambda b, g, qi: (0, 0, 0)),
                  pl.BlockSpec(cb.shape, lambda b, g, qi: (0, 0, 0)),
                  pl.BlockSpec(wb.shape, lambda b, g, qi: (0, 0, 0))],
        out_specs=pl.BlockSpec((tq, qw), lambda b, g, qi: (b * nq + qi, g)),
        out_shape=jax.ShapeDtypeStruct((batch * t, NSA_HEADS * NSA_DH), BF16),
        scratch_shapes=[pltpu.VMEM((t, NSA_DH), BF16)] * 4
        + [pltpu.VMEM((NSA_HPG * tq, 1), F32), pltpu.VMEM((NSA_HPG * tq, 1), F32),
           pltpu.VMEM((NSA_HPG * tq, NSA_DH), F32)],
        compiler_params=_cparams("parallel", "parallel", "arbitrary"),
    )(proj, proj, kc, kc, proj, proj, proj, proj, a_t, e, cb, wb)


PAGES_PER_STEP = 8


def _cache_view(cache):
    return cache.reshape(cache.shape[:-4] + (cache.shape[-4] * KV_PER_ROW, NSA_DH))


def _page_specs(layer, page_rows):
    def spec(k):
        return pl.BlockSpec((None, None, page_rows * KV_PER_ROW, NSA_DH),
                            lambda b, i, pt: (layer, pt[b, i * PAGES_PER_STEP + k], 0, 0))
    return [spec(k) for k in range(PAGES_PER_STEP)]


def _compress_sample_kernel(pt_ref, *refs):
    _compress_kernel(*refs, n_src=PAGES_PER_STEP)


def _compress_sample(pool, layer, page_table, cmp_w):
    batch, n_pages = page_table.shape
    page_rows = pool.shape[2]
    segs = page_rows // CMP_STRIDE
    s_tot = PAGES_PER_STEP * segs
    steps = n_pages // PAGES_PER_STEP
    grid_spec = pltpu.PrefetchScalarGridSpec(
        num_scalar_prefetch=1, grid=(batch, steps),
        in_specs=_page_specs(layer, page_rows) + _compress_specs(lambda f: f),
        out_specs=pl.BlockSpec((None, s_tot, 2 * KV_W), lambda b, i, pt: (b, i, 0)),
        scratch_shapes=_compress_scratch(s_tot))
    return pl.pallas_call(
        _compress_sample_kernel,
        grid_spec=grid_spec,
        out_shape=jax.ShapeDtypeStruct((batch, steps * s_tot, 2 * KV_W), BF16),
        compiler_params=_cparams("parallel", "arbitrary"),
    )(page_table, *([_cache_view(pool)] * PAGES_PER_STEP), *cmp_w)


def _stack_heads(q_ref, g):
    return jnp.concatenate([q_ref[:, (g * NSA_HPG + p) * NSA_DH:(g * NSA_HPG + p + 1) * NSA_DH]
                            for p in range(NSA_HPG)], axis=0).astype(BF16)


def _nsa_sample_select_kernel(q_ref, kck_ref, kcv_ref, at_ref, oc_ref, selt_ref, pg_ref, sc_ref, rank_ref, *,
                              past_len, n_sel):
    rq = SAMPLE_ROWS
    rows = NSA_HPG * rq
    n_slot = kck_ref.shape[0]
    pg_ref[...] = jnp.zeros_like(pg_ref)
    slot = lax.broadcasted_iota(jnp.int32, (rows, n_slot), 1)
    t_q = past_len + (lax.broadcasted_iota(jnp.int32, (rows, n_slot), 0) & (rq - 1))
    valid = (slot >= 1) & (slot * CMP_STRIDE + (CMP_BLOCK - CMP_STRIDE - 1) <= t_q)
    for g in range(NSA_KV_HEADS):
        q4 = _stack_heads(q_ref, g)
        p_c = _masked_softmax(_dot_nt(q4, kck_ref[:, g * NSA_DH:(g + 1) * NSA_DH]) * ATT_SCALE, valid)
        o_c = _dot(p_c.astype(BF16), kcv_ref[:, g * NSA_DH:(g + 1) * NSA_DH])
        pg = p_c[0:rq]
        for p in range(NSA_HPG):
            col = (g * NSA_HPG + p) * NSA_DH
            oc_ref[:, col:col + NSA_DH] = o_c[p * rq:(p + 1) * rq]
            if p:
                pg = pg + p_c[p * rq:(p + 1) * rq]
        pg_ref[g * rq:(g + 1) * rq, :] = pg
    hi, lo = _split_bf16(pg_ref[...])
    psl_t = _dot_nt(at_ref[...], hi) + _dot_nt(at_ref[...], lo)
    jshape = psl_t.shape
    t_pos = past_len + (lax.broadcasted_iota(jnp.int32, jshape, 1) & (rq - 1))
    sc_ref[...] = _selection_scores(psl_t, t_pos)
    rank_ref[...] = jnp.zeros_like(rank_ref)

    def body(jp, carry):
        j = lax.broadcasted_iota(jnp.int32, jshape, 0)
        rank_ref[...] += _beats(sc_ref[pl.ds(jp, 1), :], jp, sc_ref[...], j)
        return carry

    lax.fori_loop(0, n_sel, body, 0)
    selt_ref[...] = jnp.where(rank_ref[...] < min(SEL_TOPK, n_sel), 1.0, 0.0).astype(selt_ref.dtype)


SEL_ROWS_SAMPLE = 384


def _nsa_sample_select(proj, kc, batch, past_len, n_sel):
    n_slot = kc.shape[1]
    a_t = _selection_sum_matrix(SEL_ROWS_SAMPLE, n_slot, n_sel)
    qw = NSA_HEADS * NSA_DH
    return pl.pallas_call(
        functools.partial(_nsa_sample_select_kernel, past_len=past_len, n_sel=n_sel),
        grid=(batch,),
        in_specs=[pl.BlockSpec((SAMPLE_ROWS, qw), lambda b: (b, COL_NQ // qw)),
                  pl.BlockSpec((None, n_slot, KV_W), lambda b: (b, 0, 0)),
                  pl.BlockSpec((None, n_slot, KV_W), lambda b: (b, 0, 1)),
                  pl.BlockSpec(a_t.shape, lambda b: (0, 0))],
        out_specs=[pl.BlockSpec((SAMPLE_ROWS, qw), lambda b: (b, 0)),
                   pl.BlockSpec((None, SEL_ROWS_SAMPLE, LANE), lambda b: (b, 0, 0))],
        out_shape=[jax.ShapeDtypeStruct((batch * SAMPLE_ROWS, qw), F32),
                   jax.ShapeDtypeStruct((batch, SEL_ROWS_SAMPLE, LANE), BF16)],
        scratch_shapes=[pltpu.VMEM((LANE, n_slot), F32), pltpu.VMEM((SEL_ROWS_SAMPLE, LANE), F32),
                        pltpu.VMEM((SEL_ROWS_SAMPLE, LANE), F32)],
        compiler_params=_cparams("parallel"),
    )(proj, kc, kc, a_t)


def _nsa_sample_attend_kernel(pt_ref, *refs, past_len, page_rows):
    pages = refs[:PAGES_PER_STEP]
    (q_ref, selt_ref, e_ref, ksn_ref, kwn_ref, win_ref, gt_ref, oc_ref, o_ref,
     m_ref, l_ref, acc_ref, pad_ref) = refs[PAGES_PER_STEP:]
    i = pl.program_id(1)
    rq = SAMPLE_ROWS
    rows = NSA_HPG * rq
    blocks_per_step = PAGES_PER_STEP * page_rows // SEL_BLOCK

    @pl.when(i == 0)
    def _():
        m_ref[...] = jnp.full(m_ref.shape, NEG_INF, F32)
        l_ref[...] = jnp.zeros(l_ref.shape, F32)
        acc_ref[...] = jnp.zeros(acc_ref.shape, F32)

    def t_of(shape):
        return past_len + (lax.broadcasted_iota(jnp.int32, shape, 0) & (rq - 1))

    def online_update(g, s, ok, v):
        s = jnp.where(ok, s, NEG_INF)
        m_old = m_ref[g]
        m_new = jnp.maximum(m_old, jnp.max(s, axis=-1, keepdims=True))
        alpha = jnp.exp(m_old - m_new)
        e = jnp.where(ok, jnp.exp(s - m_new), 0.0)
        l_ref[g] = l_ref[g] * alpha + jnp.sum(e, axis=-1, keepdims=True)
        acc_ref[g] = acc_ref[g] * alpha + _dot(e.astype(BF16), v)
        m_ref[g] = m_new

    eye = jnp.where(lax.broadcasted_iota(jnp.int32, (LANE, LANE), 0) == lax.broadcasted_iota(jnp.int32, (LANE, LANE), 1),
                    1.0, 0.0).astype(BF16)

    j0 = pl.multiple_of(i * blocks_per_step, blocks_per_step)
    sel = _dot_nt(eye, selt_ref[pl.ds(j0, LANE), :]).astype(BF16)
    picked = _dot(sel, e_ref[...])
    for g in range(NSA_KV_HEADS):
        kg = jnp.concatenate([_kv_rows(pg, 0, page_rows, 0, g) for pg in pages], axis=0).astype(BF16)
        vg = jnp.concatenate([_kv_rows(pg, 0, page_rows, 1, g) for pg in pages], axis=0).astype(BF16)
        ok = jnp.concatenate([picked[g * rq:(g + 1) * rq]] * NSA_HPG, axis=0) > 0.5
        online_update(g, _dot_nt(_stack_heads(q_ref, g), kg) * ATT_SCALE, ok, vg)

    @pl.when(i == pl.num_programs(1) - 1)
    def _():
        n_past_blocks = past_len // SEL_BLOCK
        pad_ref[...] = jnp.zeros_like(pad_ref)
        pad_ref[0:rq, :] = ksn_ref[...]
        sel_new = _dot_nt(eye, selt_ref[n_past_blocks:n_past_blocks + LANE, :])
        kpos = past_len + lax.broadcasted_iota(jnp.int32, (rows, LANE), 1)
        causal = kpos <= t_of((rows, LANE))
        o_s = []
        for g in range(NSA_KV_HEADS):
            kn = pad_ref[:, g * NSA_DH:(g + 1) * NSA_DH].astype(BF16)
            vn = pad_ref[:, KV_W + g * NSA_DH:KV_W + (g + 1) * NSA_DH].astype(BF16)
            pk = jnp.concatenate([sel_new[g * rq:(g + 1) * rq, 0:1]] * NSA_HPG, axis=0)
            ok = jnp.where(causal, pk, 0.0) > 0.5
            online_update(g, _dot_nt(_stack_heads(q_ref, g), kn) * ATT_SCALE, ok, vn)
            o_s.append(acc_ref[g] / jnp.maximum(l_ref[g], 1e-30))

        pad_ref[0:rq, :] = kwn_ref[...]
        n_buf = win_ref.shape[0] // KV_PER_ROW
        kposw = past_len - n_buf + lax.broadcasted_iota(jnp.int32, (rows, n_buf + LANE), 1)
        tw = t_of((rows, n_buf + LANE))
        okw = jnp.where(kposw <= tw, jnp.where(kposw > tw - WINDOW, 1.0, 0.0), 0.0) > 0.5
        gt = jax.nn.sigmoid(gt_ref[...])
        for g in range(NSA_KV_HEADS):
            kw = jnp.concatenate([_kv_rows(win_ref, 0, n_buf, 0, g), _kv_rows(pad_ref, 0, LANE, 0, g)],
                                 axis=0).astype(BF16)
            vw = jnp.concatenate([_kv_rows(win_ref, 0, n_buf, 1, g), _kv_rows(pad_ref, 0, LANE, 1, g)],
                                 axis=0).astype(BF16)
            p_w = _masked_softmax(_dot_nt(_stack_heads(q_ref, g), kw) * ATT_SCALE, okw)
            o_w = _dot(p_w.astype(BF16), vw)
            for p in range(NSA_HPG):
                col = (g * NSA_HPG + p) * NSA_DH
                r = slice(p * rq, (p + 1) * rq)
                gc = g * LANE + p
                y = (gt[:, gc:gc + 1] * oc_ref[:, col:col + NSA_DH]
                     + gt[:, gc + NSA_HPG:gc + NSA_HPG + 1] * o_s[g][r]
                     + gt[:, gc + 2 * NSA_HPG:gc + 2 * NSA_HPG + 1] * o_w[r])
                o_ref[:, col:col + NSA_DH] = y


def _nsa_sample_attend(proj, o_c, sel_t, sel_pool, win_kv, layer, page_table, past_len):
    batch, n_pages = page_table.shape
    page_rows = sel_pool.shape[2]
    steps = n_pages // PAGES_PER_STEP
    keys = PAGES_PER_STEP * page_rows
    e = _block_expand_matrix(1, LANE, keys, keys // SEL_BLOCK)[0]
    n_buf = win_kv.shape[2]
    qw = NSA_HEADS * NSA_DH
    rq = SAMPLE_ROWS
    grid_spec = pltpu.PrefetchScalarGridSpec(
        num_scalar_prefetch=1, grid=(batch, steps),
        in_specs=_page_specs(layer, page_rows) + [
            pl.BlockSpec((rq, qw), lambda b, i, pt: (b, COL_NQ // qw)),
            pl.BlockSpec((None, SEL_ROWS_SAMPLE, LANE), lambda b, i, pt: (b, 0, 0)),
            pl.BlockSpec(e.shape, lambda b, i, pt: (0, 0)),
            pl.BlockSpec((rq, 2 * KV_W), lambda b, i, pt: (b, COL_KVS // (2 * KV_W))),
            pl.BlockSpec((rq, 2 * KV_W), lambda b, i, pt: (b, COL_KVW // (2 * KV_W))),
            pl.BlockSpec((None, None, n_buf * KV_PER_ROW, NSA_DH), lambda b, i, pt: (layer, b, 0, 0)),
            pl.BlockSpec((rq, NSA_KV_HEADS * LANE), lambda b, i, pt: (b, COL_NG // (NSA_KV_HEADS * LANE))),
            pl.BlockSpec((rq, qw), lambda b, i, pt: (b, 0))],
        out_specs=pl.BlockSpec((rq, qw), lambda b, i, pt: (b, 0)),
        scratch_shapes=[pltpu.VMEM((NSA_KV_HEADS, NSA_HPG * rq, 1), F32),
                        pltpu.VMEM((NSA_KV_HEADS, NSA_HPG * rq, 1), F32),
                        pltpu.VMEM((NSA_KV_HEADS, NSA_HPG * rq, NSA_DH), F32),
                        pltpu.VMEM((LANE, 2 * KV_W), F32)])
    return pl.pallas_call(
        functools.partial(_nsa_sample_attend_kernel, past_len=past_len, page_rows=page_rows),
        grid_spec=grid_spec,
        out_shape=jax.ShapeDtypeStruct((batch * rq, qw), F32),
        compiler_params=_cparams("parallel", "arbitrary"),
    )(page_table, *([_cache_view(sel_pool)] * PAGES_PER_STEP), proj, sel_t, e, proj, proj, _cache_view(win_kv), proj, o_c)


def _prep_gate_columns(w_in):
    ng = w_in[:, :, COL_NG:COL_NG + 3 * NSA_HEADS].reshape(DEPTH, D_MODEL, 3, NSA_KV_HEADS, NSA_HPG)
    ng = ng.transpose(0, 1, 3, 2, 4).reshape(DEPTH, D_MODEL, NSA_KV_HEADS, 3 * NSA_HPG)
    ng = jnp.pad(ng, [(0, 0), (0, 0), (0, 0), (0, LANE - 3 * NSA_HPG)])
    return ng.reshape(DEPTH, D_MODEL, NSA_KV_HEADS * LANE).astype(BF16)


def _kv_outputs_kernel(*refs, n_tiles, n_win):
    srcs, (oc_ref, os_ref, ow_ref) = refs[:-3], refs[-3:]
    layer = pl.program_id(0)
    i = pl.program_id(2)

    def relayout(src, dst):
        for c in range(2):
            for g in range(NSA_KV_HEADS):
                dst[:, c, g, :] = _kv_rows(src, 0, src.shape[0], c, g)

    for l in range(DEPTH):
        @pl.when(layer == l)
        def _(l=l):
            relayout(srcs[3 * l], oc_ref)
            relayout(srcs[3 * l + 1], os_ref)

            @pl.when(i >= n_tiles - n_win)
            def _():
                relayout(srcs[3 * l + 2], ow_ref)


def _kv_outputs(projs, batch, t, win_buf):
    tile = 512
    assert t % tile == 0 and win_buf % tile == 0
    n_tiles, n_win = t // tile, win_buf // tile
    w = 2 * KV_W
    src_spec = lambda col: pl.BlockSpec((tile, w), lambda l, b, i: (b * n_tiles + i, col // w))
    out_block = (None, None, tile, 2, NSA_KV_HEADS, NSA_DH)
    full = pl.BlockSpec(out_block, lambda l, b, i: (l, b, i, 0, 0, 0))
    last = pl.BlockSpec(out_block, lambda l, b, i: (l, b, jnp.maximum(i - (n_tiles - n_win), 0), 0, 0, 0))
    shape = lambda rows: jax.ShapeDtypeStruct((DEPTH, batch, rows, 2, NSA_KV_HEADS, NSA_DH), F32)
    return pl.pallas_call(
        functools.partial(_kv_outputs_kernel, n_tiles=n_tiles, n_win=n_win),
        grid=(DEPTH, batch, n_tiles),
        in_specs=[src_spec(col) for _ in range(DEPTH) for col in (COL_KVC, COL_KVS, COL_KVW)],
        out_specs=[full, full, last],
        out_shape=[shape(t), shape(t), shape(win_buf)],
        compiler_params=_cparams("parallel", "parallel", "arbitrary"),
    )(*[p for p in projs for _ in range(3)])


def _tile_mods(mod_rows, sub, tiles_per_seq=None, rows_per_seq=None):
    out = []
    for k in range(3):
        m = mod_rows[:, (sub * 3 + k) * D_MODEL:(sub * 3 + k + 1) * D_MODEL]
        if tiles_per_seq is not None:
            out.append(jnp.repeat(m, tiles_per_seq, axis=0)[:, None, :])
        else:
            out.append(jnp.repeat(m, rows_per_seq, axis=0)[None])
    return out


def kernel(x_prompt, x_sample, cache_cmp_kv, cache_sel_kv, cache_win_kv, state_ret, page_table, c_prompt, c_sample,
           norm_g, w_ada, b_ada, w_in, w_out, cmp_pe, cmp_w1, cmp_b1, cmp_w2, ffn_w_gate, ffn_w_up, ffn_w_down,
           final_g):
    bp, t, _ = x_prompt.shape
    bs, ts, _ = x_sample.shape
    win_buf = cache_win_kv.shape[2]
    n_pool, page_rows = cache_cmp_kv.shape[1:3]
    past_len = page_table.shape[1] * page_rows
    assert ts <= SAMPLE_ROWS and ts < CMP_STRIDE and past_len % CMP_STRIDE == 0
    assert page_table.shape[1] % PAGES_PER_STEP == 0 and past_len % SEL_BLOCK == 0
    n_sel_s = -(-(past_len + ts) // SEL_BLOCK)
    assert n_sel_s <= past_len // SEL_BLOCK + 1 and past_len // SEL_BLOCK + LANE <= SEL_ROWS_SAMPLE

    c_all = jnp.concatenate([c_prompt, c_sample], axis=0)
    c_all = jnp.pad(c_all, [(0, -c_all.shape[0] % 8), (0, 0)])
    mod = _modulation(c_all, w_ada, b_ada)

    bm = 512
    bm_p = 1024
    xp = x_prompt.reshape(bp * t, D_MODEL)
    pos_p = jnp.arange(t, dtype=jnp.int32)
    s0_p = jnp.zeros((bp, RET_HEADS, RET_DK, RET_DV), F32)
    rq = SAMPLE_ROWS
    rows_s = bs * rq
    xs = jnp.pad(x_sample, [(0, 0), (0, rq - ts), (0, 0)]).reshape(rows_s, D_MODEL)
    pos_s = past_len + jnp.arange(rq, dtype=jnp.int32)
    projs, pr, sc, ss, sw, sr = ([] for _ in range(6))
    ffn_w = (ffn_w_gate, ffn_w_up, ffn_w_down)
    w_gate = _prep_gate_columns(w_in)
    for l in range(DEPTH):
        cmp_w = _compress_weights(cmp_pe[l], cmp_w1[l], cmp_b1[l], cmp_w2[l])
        mod_p = mod[l, :bp]
        mod_s = mod[l, bp:bp + bs]

        m0 = _tile_mods(mod_s, 0, rows_per_seq=rq)
        xs, ffn_b = _ffn(xs, norm_g[l, 0], m0, ffn_w, rows_s, layer_half=(l, 0))
        m0 = _tile_mods(mod_p, 0, tiles_per_seq=t // bm)
        xp = _ffn(xp, norm_g[l, 0], m0, ffn_b, bm)

        m1s = _tile_mods(mod_s, 1, rows_per_seq=rq)
        proj_s, w_main = _proj(xs, norm_g[l, 1], m1s, w_in, w_gate, l, rows_s, cast=True)
        m1 = _tile_mods(mod_p, 1, tiles_per_seq=t // bm_p)
        proj = _proj(xp, norm_g[l, 1], m1, w_main, w_gate, l, bm_p)

        y_ret, s_ret_s = _retention(proj_s, state_ret[l], pos_s, bs, rq, ts, rq, F32)
        kc_s = _compress_sample(cache_cmp_kv, l, page_table, cmp_w)
        o_c, sel_t = _nsa_sample_select(proj_s, kc_s, bs, past_len, n_sel_s)
        y_nsa = _nsa_sample_attend(proj_s, o_c, sel_t, cache_sel_kv, cache_win_kv, l, page_table, past_len)
        xs, w_out_b = _outproj(xs, y_ret, y_nsa, m1s[2], (w_out, w_out), rows_s, layer=l)

        y_ret, s_ret = _retention(proj, s0_p, pos_p, bp, t, RET_CHUNK, RET_CHUNK, BF16)
        kc = _compress_prompt(proj, cmp_w, bp, t)
        y_nsa = _nsa_prompt(proj, kc, bp, t)
        xp = _outproj(xp, y_ret, y_nsa, m1[2], w_out_b, bm_p)
        projs.append(proj)
        pr.append(s_ret)

        m2 = _tile_mods(mod_s, 2, rows_per_seq=rq)
        xs, ffn_b = _ffn(xs, norm_g[l, 2], m2, ffn_w, rows_s, layer_half=(l, 1))
        m2 = _tile_mods(mod_p, 2, tiles_per_seq=t // bm)
        xp = _ffn(xp, norm_g[l, 2], m2, ffn_b, bm)

        new_kv = lambda c0: proj_s[:, c0:c0 + 2 * KV_W].reshape(bs, rq, 2, NSA_KV_HEADS, NSA_DH)[:, :ts]
        sc.append(new_kv(COL_KVC))
        ss.append(new_kv(COL_KVS))
        sw.append(jnp.concatenate([cache_win_kv[l], new_kv(COL_KVW)], axis=1)[:, ts:])
        sr.append(s_ret_s)

    pc, ps, pw = _kv_outputs(projs, bp, t, win_buf)
    y_prompt = _final_norm(xp, final_g, bm).reshape(bp, t, D_MODEL)
    y_sample = _final_norm(xs, final_g, rows_s).reshape(bs, rq, D_MODEL)[:, :ts]
    return (y_prompt, y_sample, pc, ps, pw, jnp.stack(pr),
            jnp.stack(sc), jnp.stack(ss), jnp.stack(sw), jnp.stack(sr))
```

```python
import functools

import jax
import jax.numpy as jnp
import numpy as np
from jax import lax
from jax.experimental import pallas as pl
from jax.experimental.pallas import tpu as pltpu

F32 = jnp.float32
BF16 = jnp.bfloat16

D_MODEL = 4096
DEPTH = 2
RET_HEADS = 8
RET_DK = 128
RET_DV = 256
RET_CHUNK = 128
ROPE_BASE = 10000.0
NSA_HEADS = 16
NSA_DH = 128
NSA_KV_HEADS = 4
NSA_HPG = NSA_HEADS // NSA_KV_HEADS
CMP_BLOCK = 32
CMP_STRIDE = 16
CMP_RATIO = CMP_BLOCK // CMP_STRIDE
CMP_HIDDEN = 2 * NSA_DH
SEL_BLOCK = 64
SEL_TOPK = 16
WINDOW = 512
N_SUB = 3
RMS_EPS = 1e-6
NEG_INF = -1e30
FORCE_SCORE = 1e9
KV_W = NSA_KV_HEADS * NSA_DH
ATT_SCALE = NSA_DH ** -0.5

COL_RQ = 0
COL_RK = COL_RQ + RET_HEADS * RET_DK
COL_RV = COL_RK + RET_HEADS * RET_DK
COL_RG = COL_RV + RET_HEADS * RET_DV
COL_NQ = COL_RG + RET_HEADS * RET_DV
COL_KVC = COL_NQ + NSA_HEADS * NSA_DH
COL_KVS = COL_KVC + 2 * KV_W
COL_KVW = COL_KVS + 2 * KV_W
COL_NG = COL_KVW + 2 * KV_W
N_PROJ = COL_NG + NSA_KV_HEADS * 128

LANE = 128
SAMPLE_ROWS = 8
VMEM_LIMIT_BYTES = 60 * 1024 * 1024


def _cparams(*sem):
    return pltpu.CompilerParams(dimension_semantics=sem, vmem_limit_bytes=VMEM_LIMIT_BYTES)


def _silu(x):
    return x * jax.nn.sigmoid(x)


def _dot(a, b):
    return jnp.dot(a, b, preferred_element_type=F32)


def _dot_nt(a, b):
    return lax.dot_general(a, b, (((1,), (1,)), ((), ())), preferred_element_type=F32)


def _masked_softmax(s, valid):
    s = jnp.where(valid, s, NEG_INF)
    m = jnp.max(s, axis=-1, keepdims=True)
    e = jnp.where(valid, jnp.exp(s - m), 0.0)
    return e / jnp.maximum(jnp.sum(e, axis=-1, keepdims=True), 1e-30)


def _mod_kernel(c_ref, w_ref, b_ref, o_ref):
    a = _silu(c_ref[...]).astype(BF16)
    o_ref[...] = _dot(a, w_ref[...].astype(BF16)) + b_ref[...]


def _modulation(c_all, w_ada, b_ada):
    rows = c_all.shape[0]
    n = w_ada.shape[-1]
    bn = 512
    return pl.pallas_call(
        _mod_kernel,
        grid=(DEPTH, n // bn),
        in_specs=[pl.BlockSpec((rows, D_MODEL), lambda l, j: (0, 0)),
                  pl.BlockSpec((None, D_MODEL, bn), lambda l, j: (l, 0, j)),
                  pl.BlockSpec((None, 1, bn), lambda l, j: (l, 0, j))],
        out_specs=pl.BlockSpec((None, rows, bn), lambda l, j: (l, 0, j)),
        out_shape=jax.ShapeDtypeStruct((DEPTH, rows, n), F32),
        compiler_params=_cparams("parallel", "parallel"),
    )(c_all, w_ada, b_ada.reshape(DEPTH, 1, n))


def _norm_modulate_into(h_ref, x_ref, g_ref, sh_ref, sc_ref):
    rows = x_ref.shape[0]
    per_row = sh_ref.shape[0] != 1
    step = min(rows, 128)
    if not per_row:
        gain = g_ref[...] * (1.0 + sc_ref[...])
    for r0 in range(0, rows, step):
        x = x_ref[r0:r0 + step, :]
        if per_row:
            gain = g_ref[...] * (1.0 + sc_ref[r0:r0 + step, :])
        sh = sh_ref[r0:r0 + step, :] if per_row else sh_ref[...]
        y = x * lax.rsqrt(jnp.mean(x * x, axis=-1, keepdims=True) + RMS_EPS)
        h_ref[r0:r0 + step, :] = (y * gain + sh).astype(h_ref.dtype)


def _mod_spec(mod_rows, bm):
    return pl.BlockSpec((None, mod_rows, D_MODEL), lambda i, j: (i, 0, 0))


def _ffn_kernel(x_ref, g_ref, sh_ref, sc_ref, gt_ref, wg_ref, wu_ref, wd_ref, o_ref, *rest, n_chunk):
    j = pl.program_id(1)
    h_ref = rest[-1]
    if len(rest) > 1:
        for src, dst in zip((wg_ref, wu_ref, wd_ref), rest[:3]):
            dst[...] = src[...].astype(BF16)
        wg_ref, wu_ref, wd_ref = rest[:3]

    @pl.when(j == 0)
    def _():
        _norm_modulate_into(h_ref, x_ref, g_ref, sh_ref, sc_ref)
        o_ref[...] = jnp.zeros_like(o_ref)

    h = h_ref[...]
    a = (_silu(_dot(h, wg_ref[...])) * _dot(h, wu_ref[...])).astype(BF16)
    for n0 in range(0, D_MODEL, n_chunk):
        o_ref[:, n0:n0 + n_chunk] += _dot(a, wd_ref[:, n0:n0 + n_chunk])

    @pl.when(j == pl.num_programs(1) - 1)
    def _():
        rows = x_ref.shape[0]
        per_row = gt_ref.shape[0] != 1
        step = min(rows, 128)
        for r0 in range(0, rows, step):
            gt = gt_ref[r0:r0 + step, :] if per_row else gt_ref[...]
            o_ref[r0:r0 + step, :] = x_ref[r0:r0 + step, :] + (0.5 * gt) * o_ref[r0:r0 + step, :]


def _ffn(x, g, mods, weights, bm, layer_half=None):
    rows = x.shape[0]
    wg, wu, wd = weights
    d_ff = wg.shape[-1]
    tf = 256
    shift, scale, gate = mods
    mr = shift.shape[1]
    up_block, down_block = (D_MODEL, tf), (tf, D_MODEL)
    up_idx, down_idx = (lambda i, j: (0, j)), (lambda i, j: (j, 0))
    out_specs = [pl.BlockSpec((bm, D_MODEL), lambda i, j: (i, 0))]
    out_shape = [jax.ShapeDtypeStruct((rows, D_MODEL), F32)]
    if layer_half is None:
        w_specs = [pl.BlockSpec(up_block, up_idx), pl.BlockSpec(up_block, up_idx), pl.BlockSpec(down_block, down_idx)]
    else:
        assert rows == bm
        l, k = layer_half
        w_specs = [pl.BlockSpec((None, None) + up_block, lambda i, j: (l, k, 0, j)),
                   pl.BlockSpec((None, None) + up_block, lambda i, j: (l, k, 0, j)),
                   pl.BlockSpec((None, None) + down_block, lambda i, j: (l, k, j, 0))]
        out_specs += [pl.BlockSpec(up_block, up_idx), pl.BlockSpec(up_block, up_idx),
                      pl.BlockSpec(down_block, down_idx)]
        out_shape += [jax.ShapeDtypeStruct((D_MODEL, d_ff), BF16)] * 2 + [jax.ShapeDtypeStruct((d_ff, D_MODEL), BF16)]
    out = pl.pallas_call(
        functools.partial(_ffn_kernel, n_chunk=512),
        grid=(rows // bm, d_ff // tf),
        in_specs=[pl.BlockSpec((bm, D_MODEL), lambda i, j: (i, 0)),
                  pl.BlockSpec((1, D_MODEL), lambda i, j: (0, 0)),
                  _mod_spec(mr, bm), _mod_spec(mr, bm), _mod_spec(mr, bm)] + w_specs,
        out_specs=out_specs,
        out_shape=out_shape,
        scratch_shapes=[pltpu.VMEM((bm, D_MODEL), BF16)],
        compiler_params=_cparams("parallel", "arbitrary"),
    )(x, g.reshape(1, D_MODEL), shift, scale, gate, wg, wu, wd)
    return out[0] if layer_half is None else (out[0], tuple(out[1:]))


def _proj_kernel(x_ref, g_ref, sh_ref, sc_ref, w_ref, wgate_ref, o_ref, *rest, n_main):
    j = pl.program_id(1)
    h_ref = rest[-1]

    @pl.when(j == 0)
    def _():
        _norm_modulate_into(h_ref, x_ref, g_ref, sh_ref, sc_ref)

    @pl.when(j < n_main)
    def _():
        if len(rest) > 1:
            rest[0][...] = w_ref[...].astype(BF16)
            o_ref[...] = _dot(h_ref[...], rest[0][...])
        else:
            o_ref[...] = _dot(h_ref[...], w_ref[...])

    @pl.when(j >= n_main)
    def _():
        o_ref[...] = _dot(h_ref[...], wgate_ref[...])


def _proj(x, g, mods, w_main, w_gate, layer, bm, cast=False):
    rows = x.shape[0]
    bn = N_PROJ - COL_NG
    n_main = COL_NG // bn
    shift, scale, _ = mods
    mr = shift.shape[1]
    main_idx = lambda i, j: (0, jnp.minimum(j, n_main - 1))
    out_specs = [pl.BlockSpec((bm, bn), lambda i, j: (i, j))]
    out_shape = [jax.ShapeDtypeStruct((rows, N_PROJ), F32)]
    if cast:
        assert rows == bm
        w_spec = pl.BlockSpec((None, D_MODEL, bn), lambda i, j: (layer, 0, jnp.minimum(j, n_main - 1)))
        out_specs.append(pl.BlockSpec((D_MODEL, bn), main_idx))
        out_shape.append(jax.ShapeDtypeStruct((D_MODEL, COL_NG), BF16))
    else:
        w_spec = pl.BlockSpec((D_MODEL, bn), main_idx)
    out = pl.pallas_call(
        functools.partial(_proj_kernel, n_main=n_main),
        grid=(rows // bm, n_main + 1),
        in_specs=[pl.BlockSpec((bm, D_MODEL), lambda i, j: (i, 0)),
                  pl.BlockSpec((1, D_MODEL), lambda i, j: (0, 0)),
                  _mod_spec(mr, bm), _mod_spec(mr, bm),
                  w_spec,
                  pl.BlockSpec((None, D_MODEL, bn), lambda i, j: (layer, 0, 0))],
        out_specs=out_specs,
        out_shape=out_shape,
        scratch_shapes=[pltpu.VMEM((bm, D_MODEL), BF16)],
        compiler_params=_cparams("parallel", "arbitrary"),
    )(x, g.reshape(1, D_MODEL), shift, scale, w_main, w_gate)
    return tuple(out) if cast else out[0]


def _outproj_kernel(x_ref, yr_ref, yn_ref, gt_ref, wa_ref, wb_ref, o_ref, *cast_out):
    if cast_out:
        for src, dst in zip((wa_ref, wb_ref), cast_out):
            dst[...] = src[...].astype(BF16)
        wa_ref, wb_ref = cast_out
    y = _dot(yr_ref[...].astype(BF16), wa_ref[...]) + _dot(yn_ref[...].astype(BF16), wb_ref[...])
    o_ref[...] = x_ref[...] + gt_ref[...] * y


def _outproj(x, y_ret, y_nsa, gate, w_halves, bm, layer=None):
    rows = x.shape[0]
    half = y_ret.shape[1]
    bn = 512
    mr = gate.shape[1]
    out_specs = [pl.BlockSpec((bm, bn), lambda i, j: (i, j))]
    out_shape = [jax.ShapeDtypeStruct((rows, D_MODEL), F32)]
    if layer is None:
        w_specs = [pl.BlockSpec((half, bn), lambda i, j: (0, j))] * 2
    else:
        assert rows == bm
        w_specs = [pl.BlockSpec((None, half, bn), lambda i, j: (layer, 0, j)),
                   pl.BlockSpec((None, half, bn), lambda i, j: (layer, 1, j))]
        out_specs += [pl.BlockSpec((half, bn), lambda i, j: (0, j))] * 2
        out_shape += [jax.ShapeDtypeStruct((half, D_MODEL), BF16)] * 2
    out = pl.pallas_call(
        _outproj_kernel,
        grid=(rows // bm, D_MODEL // bn),
        in_specs=[pl.BlockSpec((bm, bn), lambda i, j: (i, j)),
                  pl.BlockSpec((bm, half), lambda i, j: (i, 0)),
                  pl.BlockSpec((bm, half), lambda i, j: (i, 0)),
                  pl.BlockSpec((None, mr, bn), lambda i, j: (i, 0, j))] + w_specs,
        out_specs=out_specs,
        out_shape=out_shape,
        compiler_params=_cparams("parallel", "arbitrary"),
    )(x, y_ret, y_nsa, gate, *w_halves)
    return out[0] if layer is None else (out[0], tuple(out[1:]))


def _final_norm_kernel(x_ref, g_ref, o_ref):
    x = x_ref[...]
    o_ref[...] = x * lax.rsqrt(jnp.mean(x * x, axis=-1, keepdims=True) + RMS_EPS) * g_ref[...]


def _final_norm(x, g, bm):
    rows = x.shape[0]
    return pl.pallas_call(
        _final_norm_kernel,
        grid=(rows // bm,),
        in_specs=[pl.BlockSpec((bm, D_MODEL), lambda i: (i, 0)),
                  pl.BlockSpec((1, D_MODEL), lambda i: (0, 0))],
        out_specs=pl.BlockSpec((bm, D_MODEL), lambda i: (i, 0)),
        out_shape=jax.ShapeDtypeStruct((rows, D_MODEL), F32),
        compiler_params=_cparams("parallel"),
    )(x, g.reshape(1, D_MODEL))


def _retention_kernel(q_ref, k_ref, v_ref, rg_ref, cos_ref, sin_ref, dm_ref, qd_ref, kd_ref, cd_ref, s0_ref,
                      y_ref, s_ref, *pad, cq):
    @pl.when(pl.program_id(1) == 0)
    def _():
        s_ref[...] = s0_ref[...]

    cos = cos_ref[...]
    sin = sin_ref[...]
    padded = cq != RET_CHUNK
    if padded:
        kp_ref, vp_ref = pad
        kp_ref[...] = jnp.zeros_like(kp_ref)
        vp_ref[...] = jnp.zeros_like(vp_ref)
    for h in range(RET_HEADS):
        q = q_ref[:, h * RET_DK:(h + 1) * RET_DK]
        k = k_ref[:, h * RET_DK:(h + 1) * RET_DK]
        q = q * cos + pltpu.roll(q, RET_DK // 2, 1) * sin
        k = (k * cos + pltpu.roll(k, RET_DK // 2, 1) * sin) * (RET_DK ** -0.5)
        v = v_ref[:, h * RET_DV:(h + 1) * RET_DV]
        if padded:
            kp_ref[0:cq, :] = k
            vp_ref[0:cq, :] = v
            k = kp_ref[...]
            v = vp_ref[...]
        qb = q.astype(BF16)
        vb = v.astype(BF16)
        att = _dot_nt(qb, k.astype(BF16)) * dm_ref[h]
        s = s_ref[h]
        o = _dot(att.astype(BF16), vb) + _dot(qb, s.astype(BF16)) * qd_ref[h]
        kd = (k * kd_ref[h]).T.astype(BF16)
        s_ref[h] = s * cd_ref[h] + _dot(kd, vb)
        o = o * lax.rsqrt(jnp.mean(o * o, axis=-1, keepdims=True) + RMS_EPS)
        y = _silu(rg_ref[:, h * RET_DV:(h + 1) * RET_DV]) * o
        y_ref[:, h * RET_DV:(h + 1) * RET_DV] = y.astype(y_ref.dtype)


def _retention_tables(pos, chunk, cq):
    half = RET_DK // 2
    inv = jnp.power(ROPE_BASE, -jnp.arange(half, dtype=F32) * 2.0 / RET_DK)
    ang = pos.astype(F32)[:, None] * inv[None, :]
    cos = jnp.cos(ang)
    sin = jnp.sin(ang)
    cos2 = jnp.concatenate([cos, cos], axis=-1)
    sin2 = jnp.concatenate([-sin, sin], axis=-1)
    log_g = jnp.log(1.0 - jnp.exp2(-5.0 - jnp.arange(RET_HEADS, dtype=F32)))
    i = jnp.arange(chunk, dtype=F32)
    diff = i[:, None] - i[None, :]
    dmask = jnp.where(diff >= 0, jnp.exp(log_g[:, None, None] * jnp.maximum(diff, 0.0)), 0.0)
    q_dec = jnp.exp(log_g[None, :] * (i[:, None] + 1.0))
    k_dec = jnp.exp(log_g[None, :] * (chunk - 1.0 - i[:, None]))
    c_dec = jnp.exp(log_g * chunk)
    dm = jnp.zeros((RET_HEADS, cq, RET_CHUNK), F32).at[:, :chunk, :chunk].set(dmask)
    qd = jnp.zeros((RET_HEADS, cq, RET_DV), F32).at[:, :chunk, :].set(
        jnp.broadcast_to(q_dec.T[:, :, None], (RET_HEADS, chunk, RET_DV)))
    kd = jnp.zeros((RET_HEADS, RET_CHUNK, RET_DK), F32).at[:, :chunk, :].set(
        jnp.broadcast_to(k_dec.T[:, :, None], (RET_HEADS, chunk, RET_DK)))
    cd = jnp.broadcast_to(c_dec[:, None, None], (RET_HEADS, 1, RET_DV))
    return cos2, sin2, dm, qd, kd, cd


def _retention(proj, s0, pos, batch, t_rows, chunk, cq, out_dtype):
    nc = t_rows // cq
    cos2, sin2, dm, qd, kd, cd = _retention_tables(pos, chunk, cq)
    hk = RET_HEADS * RET_DK
    hv = RET_HEADS * RET_DV
    row = lambda b, c: b * nc + c
    scratch = [] if cq == RET_CHUNK else [pltpu.VMEM((RET_CHUNK, RET_DK), F32), pltpu.VMEM((RET_CHUNK, RET_DV), F32)]
    full3 = lambda shape: pl.BlockSpec(shape, lambda b, c: (0, 0, 0))
    return pl.pallas_call(
        functools.partial(_retention_kernel, cq=cq),
        grid=(batch, nc),
        in_specs=[pl.BlockSpec((cq, hk), lambda b, c: (row(b, c), COL_RQ // hk)),
                  pl.BlockSpec((cq, hk), lambda b, c: (row(b, c), COL_RK // hk)),
                  pl.BlockSpec((cq, hv), lambda b, c: (row(b, c), COL_RV // hv)),
                  pl.BlockSpec((cq, hv), lambda b, c: (row(b, c), COL_RG // hv)),
                  pl.BlockSpec((cq, RET_DK), lambda b, c: (c, 0)),
                  pl.BlockSpec((cq, RET_DK), lambda b, c: (c, 0)),
                  full3(dm.shape), full3(qd.shape), full3(kd.shape), full3(cd.shape),
                  pl.BlockSpec((None, RET_HEADS, RET_DK, RET_DV), lambda b, c: (b, 0, 0, 0))],
        out_specs=[pl.BlockSpec((cq, hv), lambda b, c: (row(b, c), 0)),
                   pl.BlockSpec((None, RET_HEADS, RET_DK, RET_DV), lambda b, c: (b, 0, 0, 0))],
        out_shape=[jax.ShapeDtypeStruct((batch * t_rows, hv), out_dtype),
                   jax.ShapeDtypeStruct((batch, RET_HEADS, RET_DK, RET_DV), F32)],
        scratch_shapes=scratch,
        compiler_params=_cparams("parallel", "arbitrary"),
    )(proj, proj, proj, proj, cos2, sin2, dm, qd, kd, cd, s0)


KV_PER_ROW = 2 * NSA_KV_HEADS


def _kv_rows(ref, r0, rows, c, g, step=1):
    if ref.shape[1] == NSA_DH:
        return ref[pl.ds(r0 * KV_PER_ROW + c * NSA_KV_HEADS + g, rows, stride=step * KV_PER_ROW), :]
    assert step == 1
    col = (c * NSA_KV_HEADS + g) * NSA_DH
    return ref[r0:r0 + rows, col:col + NSA_DH]


def _compress_kernel(*refs, n_src):
    srcs = refs[:n_src]
    perm_ref, w1_ref, pe_ref, b1_ref, w2_ref, o_ref, x_ref, prev_ref = refs[n_src:]
    cache_view = srcs[0].shape[1] == NSA_DH
    src_tokens = srcs[0].shape[0] // (KV_PER_ROW if cache_view else 1)
    s_src = src_tokens // CMP_STRIDE
    s_tot = n_src * s_src
    rows = NSA_KV_HEADS * s_tot

    @pl.when(pl.program_id(1) == 0)
    def _():
        prev_ref[...] = jnp.zeros_like(prev_ref)

    for si, src in enumerate(srcs):
        for c in range(2):
            for g in range(NSA_KV_HEADS):
                r0 = g * s_tot + si * s_src
                if cache_view:
                    for s in range(CMP_STRIDE):
                        x_ref[c, r0:r0 + s_src, s * NSA_DH:(s + 1) * NSA_DH] = _kv_rows(src, s, s_src, c, g, CMP_STRIDE)
                    continue
                grp = perm_ref.shape[0]
                segs = grp // CMP_STRIDE
                for gi in range(src_tokens // grp):
                    blk = _dot(perm_ref[...], _kv_rows(src, gi * grp, grp, c, g).astype(BF16))
                    for s in range(CMP_STRIDE):
                        x_ref[c, r0 + gi * segs:r0 + (gi + 1) * segs, s * NSA_DH:(s + 1) * NSA_DH] = (
                            blk[s * segs:(s + 1) * segs, :])

    first = (lax.broadcasted_iota(jnp.int32, (rows, CMP_HIDDEN), 0) & (s_tot - 1)) == 0
    for c in range(2):
        w1 = w1_ref[c]
        part = _dot(x_ref[c].astype(BF16), w1)
        pe_part = _dot(pe_ref[c], w1)
        bias = b1_ref[c] + pe_part[0:1, :CMP_HIDDEN] + pe_part[1:2, CMP_HIDDEN:]
        p0 = part[:, :CMP_HIDDEN]
        p1 = part[:, CMP_HIDDEN:]
        shifted = jnp.where(first, pltpu.roll(prev_ref[c], rows - s_tot + 1, 0), pltpu.roll(p0, 1, 0))
        prev_ref[c] = p0
        kc = _dot(_silu(bias + shifted + p1).astype(BF16), w2_ref[c])
        for g in range(NSA_KV_HEADS):
            col = (c * NSA_KV_HEADS + g) * NSA_DH
            o_ref[:, col:col + NSA_DH] = kc[g * s_tot:(g + 1) * s_tot].astype(o_ref.dtype)


def _compress_weights(cmp_pe, cmp_w1, cmp_b1, cmp_w2):
    w1 = cmp_w1.reshape(2, CMP_RATIO, CMP_STRIDE, NSA_DH, CMP_HIDDEN).transpose(0, 2, 3, 1, 4)
    w1 = w1.reshape(2, CMP_STRIDE * NSA_DH, CMP_RATIO * CMP_HIDDEN).astype(BF16)
    pe = cmp_pe.reshape(2, CMP_RATIO, CMP_STRIDE * NSA_DH)
    pe = jnp.pad(pe, [(0, 0), (0, 8 - CMP_RATIO), (0, 0)]).astype(BF16)
    return _segment_permutation(), w1, pe, cmp_b1.reshape(2, 1, CMP_HIDDEN), cmp_w2.astype(BF16)


COMPRESS_GROUP_ROWS = 128


def _segment_permutation():
    segs = COMPRESS_GROUP_ROWS // CMP_STRIDE
    r = np.arange(COMPRESS_GROUP_ROWS)
    src = (r % segs) * CMP_STRIDE + r // segs
    return jnp.asarray(src[:, None] == np.arange(COMPRESS_GROUP_ROWS)[None, :], BF16)


def _compress_specs(idx):
    kdim = CMP_STRIDE * NSA_DH
    return [pl.BlockSpec((COMPRESS_GROUP_ROWS, COMPRESS_GROUP_ROWS), idx(lambda *_: (0, 0))),
            pl.BlockSpec((2, kdim, CMP_RATIO * CMP_HIDDEN), idx(lambda *_: (0, 0, 0))),
            pl.BlockSpec((2, 8, kdim), idx(lambda *_: (0, 0, 0))),
            pl.BlockSpec((2, 1, CMP_HIDDEN), idx(lambda *_: (0, 0, 0))),
            pl.BlockSpec((2, CMP_HIDDEN, NSA_DH), idx(lambda *_: (0, 0, 0)))]


def _compress_scratch(s_tot):
    rows = NSA_KV_HEADS * s_tot
    return [pltpu.VMEM((2, rows, CMP_STRIDE * NSA_DH), F32), pltpu.VMEM((2, rows, CMP_HIDDEN), F32)]


def _compress_prompt(proj, cmp_w, batch, t):
    segs = t // CMP_STRIDE
    return pl.pallas_call(
        functools.partial(_compress_kernel, n_src=1),
        grid=(batch, 1),
        in_specs=[pl.BlockSpec((t, 2 * KV_W), lambda b, i: (b, COL_KVC // (2 * KV_W)))]
        + _compress_specs(lambda f: f),
        out_specs=pl.BlockSpec((None, segs, 2 * KV_W), lambda b, i: (b, 0, 0)),
        out_shape=jax.ShapeDtypeStruct((batch, segs, 2 * KV_W), BF16),
        scratch_shapes=_compress_scratch(segs),
        compiler_params=_cparams("parallel", "arbitrary"),
    )(proj, *cmp_w)


def _split_bf16(x):
    hi = x.astype(BF16)
    return hi, (x - hi.astype(F32)).astype(BF16)


def _selection_scores(psl_t, t_pos):
    j = lax.broadcasted_iota(jnp.int32, psl_t.shape, 0)
    jc = t_pos // SEL_BLOCK
    forced = (j == 0) | (j == jc) | (j == jc - 1)
    return jnp.where(j > jc, NEG_INF, jnp.where(forced, FORCE_SCORE, psl_t))


def _beats(row, jp, score, j):
    return jnp.where(row > score, 1.0, jnp.where(row == score, jnp.where(j > jp, 1.0, 0.0), 0.0))


def _nsa_prompt_kernel(q_ref, gt_ref, kck_ref, kcv_ref, ks_ref, vs_ref, kw_ref, vw_ref, at_ref, e_ref, cb_ref, wb_ref,
                       o_ref, ksb, vsb, kwb, vwb, m_ref, l_ref, acc_ref, *, tq, tk, n_sel):
    qi = pl.program_id(2)
    rows = NSA_HPG * tq
    groups = range(NSA_GROUPS_PER_STEP)
    dh = lambda u: slice(u * NSA_DH, (u + 1) * NSA_DH)

    @pl.when(qi == 0)
    def _():
        for u in groups:
            ksb[u] = ks_ref[:, dh(u)].astype(BF16)
            vsb[u] = vs_ref[:, dh(u)].astype(BF16)
            kwb[u] = kw_ref[:, dh(u)].astype(BF16)
            vwb[u] = vw_ref[:, dh(u)].astype(BF16)

    def t_of(shape):
        return qi * tq + (lax.broadcasted_iota(jnp.int32, shape, 0) & (tq - 1))

    q4 = [(jnp.concatenate([q_ref[:, (u * NSA_HPG + p) * NSA_DH:(u * NSA_HPG + p + 1) * NSA_DH]
                            for p in range(NSA_HPG)], axis=0) * ATT_SCALE).astype(BF16) for u in groups]

    n_slot = kck_ref.shape[0]
    slot = lax.broadcasted_iota(jnp.int32, (rows, n_slot), 1)
    valid = (slot >= 1) & (slot * CMP_STRIDE + (CMP_BLOCK - CMP_STRIDE - 1) <= t_of((rows, n_slot)))
    p_c = [_masked_softmax(_dot_nt(q4[u], kck_ref[:, dh(u)]), valid) for u in groups]
    o_c = [_dot(p_c[u].astype(BF16), kcv_ref[:, dh(u)]) for u in groups]

    eye = jnp.where(lax.broadcasted_iota(jnp.int32, (tq, tq), 0) == lax.broadcasted_iota(jnp.int32, (tq, tq), 1),
                    1.0, 0.0).astype(BF16)
    sel_bias = []
    for u in groups:
        pg = p_c[u][0:tq]
        for p in range(1, NSA_HPG):
            pg = pg + p_c[u][p * tq:(p + 1) * tq]
        hi, lo = _split_bf16(pg)
        psl_t = _dot_nt(at_ref[...], hi) + _dot_nt(at_ref[...], lo)
        jshape = psl_t.shape
        j = lax.broadcasted_iota(jnp.int32, jshape, 0)
        score = _selection_scores(psl_t, qi * tq + lax.broadcasted_iota(jnp.int32, jshape, 1))
        rank = jnp.zeros(jshape, F32)
        for jp in range(n_sel):
            rank = rank + _beats(score[jp:jp + 1, :], jp, score, j)
        bias_t = jnp.where((rank < min(SEL_TOPK, n_sel)) & (j < n_sel), 0.0, NEG_INF)
        bias_t = jnp.concatenate([bias_t, jnp.full((e_ref.shape[1] - jshape[0], tq), NEG_INF, F32)], axis=0)
        sel_bias.append(_dot_nt(eye, bias_t.astype(BF16)).astype(BF16))

    m_ref[...] = jnp.full(m_ref.shape, NEG_INF, F32)
    l_ref[...] = jnp.zeros(l_ref.shape, F32)
    acc_ref[...] = jnp.zeros(acc_ref.shape, F32)

    def chunk(c, causal_bias):
        k0 = pl.multiple_of(c * tk, tk)
        e_c = e_ref[c]
        keys = [ksb[u, pl.ds(k0, tk), :] for u in groups]
        vals = [vsb[u, pl.ds(k0, tk), :] for u in groups]
        old = [(m_ref[u], l_ref[u], acc_ref[u]) for u in groups]
        new = []
        for u in groups:
            bias = _dot(sel_bias[u], e_c)
            if causal_bias is not None:
                bias = bias + causal_bias
            s = _dot_nt(q4[u], keys[u]) + jnp.concatenate([bias] * NSA_HPG, axis=0)
            m_old, l_old, acc_old = old[u]
            m_new = jnp.maximum(m_old, jnp.max(s, axis=-1, keepdims=True))
            alpha = jnp.exp(m_old - m_new)
            e = jnp.exp(s - m_new)
            new.append((m_new, l_old * alpha + jnp.sum(e, axis=-1, keepdims=True),
                        acc_old * alpha + _dot(e.astype(BF16), vals[u])))
        for u in groups:
            m_ref[u], l_ref[u], acc_ref[u] = new[u]

    n_full = (qi * tq) // tk

    def full_chunk(c, carry):
        chunk(c, None)
        return carry

    lax.fori_loop(0, n_full, full_chunk, 0)
    chunk(n_full, cb_ref[qi % (tk // tq)])

    n_back = WINDOW // tq
    w_len = wb_ref.shape[2]
    qw_i = jnp.minimum(qi, n_back)
    w0 = pl.multiple_of((qi - qw_i) * tq, tq)
    w_bias = jnp.concatenate([wb_ref[qw_i]] * NSA_HPG, axis=0)
    gt = jax.nn.sigmoid(gt_ref[...])
    for u in groups:
        o_s = acc_ref[u] / jnp.maximum(l_ref[u], 1e-30)
        s_w = _dot_nt(q4[u], kwb[u, pl.ds(w0, w_len), :]) + w_bias
        e_w = jnp.exp(s_w - jnp.max(s_w, axis=-1, keepdims=True))
        p_w = e_w / jnp.maximum(jnp.sum(e_w, axis=-1, keepdims=True), 1e-30)
        o_w = _dot(p_w.astype(BF16), vwb[u, pl.ds(w0, w_len), :])
        for p in range(NSA_HPG):
            r = slice(p * tq, (p + 1) * tq)
            gc = u * LANE + p
            y = (gt[:, gc:gc + 1] * o_c[u][r] + gt[:, gc + NSA_HPG:gc + NSA_HPG + 1] * o_s[r]
                 + gt[:, gc + 2 * NSA_HPG:gc + 2 * NSA_HPG + 1] * o_w[r])
            col = (u * NSA_HPG + p) * NSA_DH
            o_ref[:, col:col + NSA_DH] = y.astype(o_ref.dtype)


def _selection_sum_matrix(n_rows, n_slot, n_sel):
    ratio = SEL_BLOCK // CMP_STRIDE
    j = np.arange(n_rows)[:, None]
    m = np.arange(n_slot)[None, :]
    a = (j < n_sel) & (m >= ratio * j) & (m <= ratio * j + ratio + CMP_RATIO - 2)
    return jnp.asarray(a, BF16)


def _block_expand_matrix(n_chunks, n_rows, tk, blocks_per_chunk):
    c = np.arange(n_chunks)[:, None, None]
    j = np.arange(n_rows)[None, :, None]
    k = np.arange(tk)[None, None, :]
    return jnp.asarray((c * blocks_per_chunk + k // SEL_BLOCK) == j, BF16)


NSA_GROUPS_PER_STEP = 1


def _causal_bias_table(tq, tk):
    d = np.arange(tk // tq)[:, None, None]
    r = np.arange(tq)[None, :, None]
    k = np.arange(tk)[None, None, :]
    return jnp.asarray(np.where(k <= d * tq + r, 0.0, NEG_INF), F32)


def _window_bias_table(tq, n_back, w_len):
    d = np.arange(n_back + 1)[:, None, None]
    r = np.arange(tq)[None, :, None]
    k = np.arange(w_len)[None, None, :]
    rel = k - d * tq - r
    return jnp.asarray(np.where((rel <= 0) & (rel > -WINDOW), 0.0, NEG_INF), F32)


def _nsa_prompt(proj, kc, batch, t):
    tq, tk = 128, 512
    nq = t // tq
    n_sel = t // SEL_BLOCK
    n_slot = kc.shape[1]
    n_back = WINDOW // tq
    a_t = _selection_sum_matrix(-(-n_sel // 8) * 8, n_slot, n_sel)
    e = _block_expand_matrix(t // tk, LANE, tk, tk // SEL_BLOCK)
    cb = _causal_bias_table(tq, tk)
    wb = _window_bias_table(tq, n_back, min((n_back + 1) * tq, t))
    n_u = NSA_GROUPS_PER_STEP
    qw = n_u * NSA_HPG * NSA_DH
    kw = n_u * NSA_DH
    kv = lambda base, c: pl.BlockSpec((t, kw), lambda b, g, qi: (b, (base + c * KV_W) // kw + g))
    return pl.pallas_call(
        functools.partial(_nsa_prompt_kernel, tq=tq, tk=tk, n_sel=n_sel),
        grid=(batch, NSA_KV_HEADS // n_u, nq),
        in_specs=[pl.BlockSpec((tq, qw), lambda b, g, qi: (b * nq + qi, COL_NQ // qw + g)),
                  pl.BlockSpec((tq, n_u * LANE), lambda b, g, qi: (b * nq + qi, COL_NG // (n_u * LANE) + g)),
                  pl.BlockSpec((None, n_slot, kw), lambda b, g, qi: (b, 0, g)),
                  pl.BlockSpec((None, n_slot, kw), lambda b, g, qi: (b, 0, KV_W // kw + g)),
                  kv(COL_KVS, 0), kv(COL_KVS, 1), kv(COL_KVW, 0), kv(COL_KVW, 1),
                  pl.BlockSpec(a_t.shape, lambda b, g, qi: (0, 0)),
                  pl.BlockSpec(e.shape, lambda b, g, qi: (0, 0, 0)),
                  pl.BlockSpec(cb.shape, lambda b, g, qi: (0, 0, 0)),
                  pl.BlockSpec(wb.shape, lambda b, g, qi: (0, 0, 0))],
        out_specs=pl.BlockSpec((tq, qw), lambda b, g, qi: (b * nq + qi, g)),
        out_shape=jax.ShapeDtypeStruct((batch * t, NSA_HEADS * NSA_DH), BF16),
        scratch_shapes=[pltpu.VMEM((n_u, t, NSA_DH), BF16)] * 4
        + [pltpu.VMEM((n_u, NSA_HPG * tq, 1), F32), pltpu.VMEM((n_u, NSA_HPG * tq, 1), F32),
           pltpu.VMEM((n_u, NSA_HPG * tq, NSA_DH), F32)],
        compiler_params=_cparams("parallel", "parallel", "arbitrary"),
    )(proj, proj, kc, kc, proj, proj, proj, proj, a_t, e, cb, wb)


PAGES_PER_STEP = 8


def _cache_view(cache):
    return cache.reshape(cache.shape[:-4] + (cache.shape[-4] * KV_PER_ROW, NSA_DH))


def _page_specs(layer, page_rows):
    def spec(k):
        return pl.BlockSpec((None, None, page_rows * KV_PER_ROW, NSA_DH),
                            lambda b, i, pt: (layer, pt[b, i * PAGES_PER_STEP + k], 0, 0))
    return [spec(k) for k in range(PAGES_PER_STEP)]


def _compress_sample_kernel(pt_ref, *refs):
    _compress_kernel(*refs, n_src=PAGES_PER_STEP)


def _compress_sample(pool, layer, page_table, cmp_w):
    batch, n_pages = page_table.shape
    page_rows = pool.shape[2]
    segs = page_rows // CMP_STRIDE
    s_tot = PAGES_PER_STEP * segs
    steps = n_pages // PAGES_PER_STEP
    grid_spec = pltpu.PrefetchScalarGridSpec(
        num_scalar_prefetch=1, grid=(batch, steps),
        in_specs=_page_specs(layer, page_rows) + _compress_specs(lambda f: f),
        out_specs=pl.BlockSpec((None, s_tot, 2 * KV_W), lambda b, i, pt: (b, i, 0)),
        scratch_shapes=_compress_scratch(s_tot))
    return pl.pallas_call(
        _compress_sample_kernel,
        grid_spec=grid_spec,
        out_shape=jax.ShapeDtypeStruct((batch, steps * s_tot, 2 * KV_W), BF16),
        compiler_params=_cparams("parallel", "arbitrary"),
    )(page_table, *([_cache_view(pool)] * PAGES_PER_STEP), *cmp_w)


def _stack_heads(q_ref, g):
    return jnp.concatenate([q_ref[:, (g * NSA_HPG + p) * NSA_DH:(g * NSA_HPG + p + 1) * NSA_DH]
                            for p in range(NSA_HPG)], axis=0).astype(BF16)


def _nsa_sample_select_kernel(q_ref, kck_ref, kcv_ref, at_ref, oc_ref, selt_ref, pg_ref, sc_ref, rank_ref, *,
                              past_len, n_sel):
    rq = SAMPLE_ROWS
    rows = NSA_HPG * rq
    n_slot = kck_ref.shape[0]
    pg_ref[...] = jnp.zeros_like(pg_ref)
    slot = lax.broadcasted_iota(jnp.int32, (rows, n_slot), 1)
    t_q = past_len + (lax.broadcasted_iota(jnp.int32, (rows, n_slot), 0) & (rq - 1))
    valid = (slot >= 1) & (slot * CMP_STRIDE + (CMP_BLOCK - CMP_STRIDE - 1) <= t_q)
    for g in range(NSA_KV_HEADS):
        q4 = _stack_heads(q_ref, g)
        p_c = _masked_softmax(_dot_nt(q4, kck_ref[:, g * NSA_DH:(g + 1) * NSA_DH]) * ATT_SCALE, valid)
        o_c = _dot(p_c.astype(BF16), kcv_ref[:, g * NSA_DH:(g + 1) * NSA_DH])
        pg = p_c[0:rq]
        for p in range(NSA_HPG):
            col = (g * NSA_HPG + p) * NSA_DH
            oc_ref[:, col:col + NSA_DH] = o_c[p * rq:(p + 1) * rq]
            if p:
                pg = pg + p_c[p * rq:(p + 1) * rq]
        pg_ref[g * rq:(g + 1) * rq, :] = pg
    hi, lo = _split_bf16(pg_ref[...])
    psl_t = _dot_nt(at_ref[...], hi) + _dot_nt(at_ref[...], lo)
    jshape = psl_t.shape
    t_pos = past_len + (lax.broadcasted_iota(jnp.int32, jshape, 1) & (rq - 1))
    sc_ref[...] = _selection_scores(psl_t, t_pos)
    rank_ref[...] = jnp.zeros_like(rank_ref)

    def body(jp, carry):
        j = lax.broadcasted_iota(jnp.int32, jshape, 0)
        rank_ref[...] += _beats(sc_ref[pl.ds(jp, 1), :], jp, sc_ref[...], j)
        return carry

    lax.fori_loop(0, n_sel, body, 0)
    selt_ref[...] = jnp.where(rank_ref[...] < min(SEL_TOPK, n_sel), 1.0, 0.0).astype(selt_ref.dtype)


SEL_ROWS_SAMPLE = 384


def _nsa_sample_select(proj, kc, batch, past_len, n_sel):
    n_slot = kc.shape[1]
    a_t = _selection_sum_matrix(SEL_ROWS_SAMPLE, n_slot, n_sel)
    qw = NSA_HEADS * NSA_DH
    return pl.pallas_call(
        functools.partial(_nsa_sample_select_kernel, past_len=past_len, n_sel=n_sel),
        grid=(batch,),
        in_specs=[pl.BlockSpec((SAMPLE_ROWS, qw), lambda b: (b, COL_NQ // qw)),
                  pl.BlockSpec((None, n_slot, KV_W), lambda b: (b, 0, 0)),
                  pl.BlockSpec((None, n_slot, KV_W), lambda b: (b, 0, 1)),
                  pl.BlockSpec(a_t.shape, lambda b: (0, 0))],
        out_specs=[pl.BlockSpec((SAMPLE_ROWS, qw), lambda b: (b, 0)),
                   pl.BlockSpec((None, SEL_ROWS_SAMPLE, LANE), lambda b: (b, 0, 0))],
        out_shape=[jax.ShapeDtypeStruct((batch * SAMPLE_ROWS, qw), F32),
                   jax.ShapeDtypeStruct((batch, SEL_ROWS_SAMPLE, LANE), BF16)],
        scratch_shapes=[pltpu.VMEM((LANE, n_slot), F32), pltpu.VMEM((SEL_ROWS_SAMPLE, LANE), F32),
                        pltpu.VMEM((SEL_ROWS_SAMPLE, LANE), F32)],
        compiler_params=_cparams("parallel"),
    )(proj, kc, kc, a_t)


def _nsa_sample_attend_kernel(pt_ref, *refs, past_len, page_rows):
    pages = refs[:PAGES_PER_STEP]
    (q_ref, selt_ref, e_ref, ksn_ref, kwn_ref, win_ref, gt_ref, oc_ref, o_ref,
     m_ref, l_ref, acc_ref, pad_ref) = refs[PAGES_PER_STEP:]
    i = pl.program_id(1)
    rq = SAMPLE_ROWS
    rows = NSA_HPG * rq
    blocks_per_step = PAGES_PER_STEP * page_rows // SEL_BLOCK

    @pl.when(i == 0)
    def _():
        m_ref[...] = jnp.full(m_ref.shape, NEG_INF, F32)
        l_ref[...] = jnp.zeros(l_ref.shape, F32)
        acc_ref[...] = jnp.zeros(acc_ref.shape, F32)

    def t_of(shape):
        return past_len + (lax.broadcasted_iota(jnp.int32, shape, 0) & (rq - 1))

    all_groups = range(NSA_KV_HEADS)
    q4 = [_stack_heads(q_ref, g) for g in all_groups]

    def per_head_rows(x, g):
        return jnp.concatenate([x[g * rq:(g + 1) * rq]] * NSA_HPG, axis=0)

    def online_update(keys, picked, values):
        s = jnp.concatenate([_dot_nt(q4[g], keys[g]) for g in all_groups], axis=0) * ATT_SCALE
        ok = jnp.concatenate([per_head_rows(picked, g) for g in all_groups], axis=0) > 0.5
        s = jnp.where(ok, s, NEG_INF)
        m_old = m_ref[...]
        m_new = jnp.maximum(m_old, jnp.max(s, axis=-1, keepdims=True))
        alpha = jnp.exp(m_old - m_new)
        e = jnp.where(ok, jnp.exp(s - m_new), 0.0)
        l_ref[...] = l_ref[...] * alpha + jnp.sum(e, axis=-1, keepdims=True)
        eb = e.astype(BF16)
        pv = jnp.concatenate([_dot(eb[g * rows:(g + 1) * rows], values[g]) for g in all_groups], axis=0)
        acc_ref[...] = acc_ref[...] * alpha + pv
        m_ref[...] = m_new

    eye = jnp.where(lax.broadcasted_iota(jnp.int32, (LANE, LANE), 0) == lax.broadcasted_iota(jnp.int32, (LANE, LANE), 1),
                    1.0, 0.0).astype(BF16)

    j0 = pl.multiple_of(i * blocks_per_step, blocks_per_step)
    sel = _dot_nt(eye, selt_ref[pl.ds(j0, LANE), :]).astype(BF16)
    picked = _dot(sel, e_ref[...])
    online_update(
        [jnp.concatenate([_kv_rows(pg, 0, page_rows, 0, g) for pg in pages], axis=0).astype(BF16) for g in all_groups],
        picked,
        [jnp.concatenate([_kv_rows(pg, 0, page_rows, 1, g) for pg in pages], axis=0).astype(BF16) for g in all_groups])

    @pl.when(i == pl.num_programs(1) - 1)
    def _():
        n_past_blocks = past_len // SEL_BLOCK
        pad_ref[...] = jnp.zeros_like(pad_ref)
        pad_ref[0:rq, :] = ksn_ref[...]
        sel_new = _dot_nt(eye, selt_ref[n_past_blocks:n_past_blocks + LANE, :])
        kpos = past_len + lax.broadcasted_iota(jnp.int32, (LANE, LANE), 1)
        causal = kpos <= t_of((LANE, LANE))
        online_update([_kv_rows(pad_ref, 0, LANE, 0, g).astype(BF16) for g in all_groups],
                      jnp.where(causal, sel_new[:, 0:1], 0.0),
                      [_kv_rows(pad_ref, 0, LANE, 1, g).astype(BF16) for g in all_groups])
        o_s_all = acc_ref[...] / jnp.maximum(l_ref[...], 1e-30)
        o_s = [o_s_all[g * rows:(g + 1) * rows] for g in all_groups]

        pad_ref[0:rq, :] = kwn_ref[...]
        n_buf = win_ref.shape[0] // KV_PER_ROW
        kposw = past_len - n_buf + lax.broadcasted_iota(jnp.int32, (rows, n_buf + LANE), 1)
        tw = t_of((rows, n_buf + LANE))
        okw = jnp.where(kposw <= tw, jnp.where(kposw > tw - WINDOW, 1.0, 0.0), 0.0) > 0.5
        gt = jax.nn.sigmoid(gt_ref[...])
        for g in range(NSA_KV_HEADS):
            kw = jnp.concatenate([_kv_rows(win_ref, 0, n_buf, 0, g), _kv_rows(pad_ref, 0, LANE, 0, g)],
                                 axis=0).astype(BF16)
            vw = jnp.concatenate([_kv_rows(win_ref, 0, n_buf, 1, g), _kv_rows(pad_ref, 0, LANE, 1, g)],
                                 axis=0).astype(BF16)
            p_w = _masked_softmax(_dot_nt(_stack_heads(q_ref, g), kw) * ATT_SCALE, okw)
            o_w = _dot(p_w.astype(BF16), vw)
            for p in range(NSA_HPG):
                col = (g * NSA_HPG + p) * NSA_DH
                r = slice(p * rq, (p + 1) * rq)
                gc = g * LANE + p
                y = (gt[:, gc:gc + 1] * oc_ref[:, col:col + NSA_DH]
                     + gt[:, gc + NSA_HPG:gc + NSA_HPG + 1] * o_s[g][r]
                     + gt[:, gc + 2 * NSA_HPG:gc + 2 * NSA_HPG + 1] * o_w[r])
                o_ref[:, col:col + NSA_DH] = y


def _nsa_sample_attend(proj, o_c, sel_t, sel_pool, win_kv, layer, page_table, past_len):
    batch, n_pages = page_table.shape
    page_rows = sel_pool.shape[2]
    steps = n_pages // PAGES_PER_STEP
    keys = PAGES_PER_STEP * page_rows
    e = _block_expand_matrix(1, LANE, keys, keys // SEL_BLOCK)[0]
    n_buf = win_kv.shape[2]
    qw = NSA_HEADS * NSA_DH
    rq = SAMPLE_ROWS
    grid_spec = pltpu.PrefetchScalarGridSpec(
        num_scalar_prefetch=1, grid=(batch, steps),
        in_specs=_page_specs(layer, page_rows) + [
            pl.BlockSpec((rq, qw), lambda b, i, pt: (b, COL_NQ // qw)),
            pl.BlockSpec((None, SEL_ROWS_SAMPLE, LANE), lambda b, i, pt: (b, 0, 0)),
            pl.BlockSpec(e.shape, lambda b, i, pt: (0, 0)),
            pl.BlockSpec((rq, 2 * KV_W), lambda b, i, pt: (b, COL_KVS // (2 * KV_W))),
            pl.BlockSpec((rq, 2 * KV_W), lambda b, i, pt: (b, COL_KVW // (2 * KV_W))),
            pl.BlockSpec((None, None, n_buf * KV_PER_ROW, NSA_DH), lambda b, i, pt: (layer, b, 0, 0)),
            pl.BlockSpec((rq, NSA_KV_HEADS * LANE), lambda b, i, pt: (b, COL_NG // (NSA_KV_HEADS * LANE))),
            pl.BlockSpec((rq, qw), lambda b, i, pt: (b, 0))],
        out_specs=pl.BlockSpec((rq, qw), lambda b, i, pt: (b, 0)),
        scratch_shapes=[pltpu.VMEM((NSA_HEADS * rq, 1), F32),
                        pltpu.VMEM((NSA_HEADS * rq, 1), F32),
                        pltpu.VMEM((NSA_HEADS * rq, NSA_DH), F32),
                        pltpu.VMEM((LANE, 2 * KV_W), F32)])
    return pl.pallas_call(
        functools.partial(_nsa_sample_attend_kernel, past_len=past_len, page_rows=page_rows),
        grid_spec=grid_spec,
        out_shape=jax.ShapeDtypeStruct((batch * rq, qw), F32),
        compiler_params=_cparams("parallel", "arbitrary"),
    )(page_table, *([_cache_view(sel_pool)] * PAGES_PER_STEP), proj, sel_t, e, proj, proj, _cache_view(win_kv), proj, o_c)


def _prep_gate_columns(w_in):
    ng = w_in[:, :, COL_NG:COL_NG + 3 * NSA_HEADS].reshape(DEPTH, D_MODEL, 3, NSA_KV_HEADS, NSA_HPG)
    ng = ng.transpose(0, 1, 3, 2, 4).reshape(DEPTH, D_MODEL, NSA_KV_HEADS, 3 * NSA_HPG)
    ng = jnp.pad(ng, [(0, 0), (0, 0), (0, 0), (0, LANE - 3 * NSA_HPG)])
    return ng.reshape(DEPTH, D_MODEL, NSA_KV_HEADS * LANE).astype(BF16)


def _kv_outputs_kernel(*refs, n_tiles, n_win):
    srcs, (oc_ref, os_ref, ow_ref) = refs[:-3], refs[-3:]
    layer = pl.program_id(0)
    i = pl.program_id(2)

    def relayout(src, dst):
        for c in range(2):
            for g in range(NSA_KV_HEADS):
                dst[:, c, g, :] = _kv_rows(src, 0, src.shape[0], c, g)

    for l in range(DEPTH):
        @pl.when(layer == l)
        def _(l=l):
            relayout(srcs[3 * l], oc_ref)
            relayout(srcs[3 * l + 1], os_ref)

            @pl.when(i >= n_tiles - n_win)
            def _():
                relayout(srcs[3 * l + 2], ow_ref)


def _kv_outputs(projs, batch, t, win_buf):
    tile = 512
    assert t % tile == 0 and win_buf % tile == 0
    n_tiles, n_win = t // tile, win_buf // tile
    w = 2 * KV_W
    src_spec = lambda col: pl.BlockSpec((tile, w), lambda l, b, i: (b * n_tiles + i, col // w))
    out_block = (None, None, tile, 2, NSA_KV_HEADS, NSA_DH)
    full = pl.BlockSpec(out_block, lambda l, b, i: (l, b, i, 0, 0, 0))
    last = pl.BlockSpec(out_block, lambda l, b, i: (l, b, jnp.maximum(i - (n_tiles - n_win), 0), 0, 0, 0))
    shape = lambda rows: jax.ShapeDtypeStruct((DEPTH, batch, rows, 2, NSA_KV_HEADS, NSA_DH), F32)
    return pl.pallas_call(
        functools.partial(_kv_outputs_kernel, n_tiles=n_tiles, n_win=n_win),
        grid=(DEPTH, batch, n_tiles),
        in_specs=[src_spec(col) for _ in range(DEPTH) for col in (COL_KVC, COL_KVS, COL_KVW)],
        out_specs=[full, full, last],
        out_shape=[shape(t), shape(t), shape(win_buf)],
        compiler_params=_cparams("parallel", "parallel", "arbitrary"),
    )(*[p for p in projs for _ in range(3)])


def _tile_mods(mod_rows, sub, tiles_per_seq=None, rows_per_seq=None):
    out = []
    for k in range(3):
        m = mod_rows[:, (sub * 3 + k) * D_MODEL:(sub * 3 + k + 1) * D_MODEL]
        if tiles_per_seq is not None:
            out.append(jnp.repeat(m, tiles_per_seq, axis=0)[:, None, :])
        else:
            out.append(jnp.repeat(m, rows_per_seq, axis=0)[None])
    return out


def kernel(x_prompt, x_sample, cache_cmp_kv, cache_sel_kv, cache_win_kv, state_ret, page_table, c_prompt, c_sample,
           norm_g, w_ada, b_ada, w_in, w_out, cmp_pe, cmp_w1, cmp_b1, cmp_w2, ffn_w_gate, ffn_w_up, ffn_w_down,
           final_g):
    bp, t, _ = x_prompt.shape
    bs, ts, _ = x_sample.shape
    win_buf = cache_win_kv.shape[2]
    n_pool, page_rows = cache_cmp_kv.shape[1:3]
    past_len = page_table.shape[1] * page_rows
    assert ts <= SAMPLE_ROWS and ts < CMP_STRIDE and past_len % CMP_STRIDE == 0
    assert page_table.shape[1] % PAGES_PER_STEP == 0 and past_len % SEL_BLOCK == 0
    n_sel_s = -(-(past_len + ts) // SEL_BLOCK)
    assert n_sel_s <= past_len // SEL_BLOCK + 1 and past_len // SEL_BLOCK + LANE <= SEL_ROWS_SAMPLE

    c_all = jnp.concatenate([c_prompt, c_sample], axis=0)
    c_all = jnp.pad(c_all, [(0, -c_all.shape[0] % 8), (0, 0)])
    mod = _modulation(c_all, w_ada, b_ada)

    bm = 512
    bm_p = 1024
    xp = x_prompt.reshape(bp * t, D_MODEL)
    pos_p = jnp.arange(t, dtype=jnp.int32)
    s0_p = jnp.zeros((bp, RET_HEADS, RET_DK, RET_DV), F32)
    rq = SAMPLE_ROWS
    rows_s = bs * rq
    xs = jnp.pad(x_sample, [(0, 0), (0, rq - ts), (0, 0)]).reshape(rows_s, D_MODEL)
    pos_s = past_len + jnp.arange(rq, dtype=jnp.int32)
    projs, pr, sc, ss, sw, sr = ([] for _ in range(6))
    ffn_w = (ffn_w_gate, ffn_w_up, ffn_w_down)
    w_gate = _prep_gate_columns(w_in)
    for l in range(DEPTH):
        cmp_w = _compress_weights(cmp_pe[l], cmp_w1[l], cmp_b1[l], cmp_w2[l])
        mod_p = mod[l, :bp]
        mod_s = mod[l, bp:bp + bs]

        m0 = _tile_mods(mod_s, 0, rows_per_seq=rq)
        xs, ffn_b = _ffn(xs, norm_g[l, 0], m0, ffn_w, rows_s, layer_half=(l, 0))
        m0 = _tile_mods(mod_p, 0, tiles_per_seq=t // bm)
        xp = _ffn(xp, norm_g[l, 0], m0, ffn_b, bm)

        m1s = _tile_mods(mod_s, 1, rows_per_seq=rq)
        proj_s, w_main = _proj(xs, norm_g[l, 1], m1s, w_in, w_gate, l, rows_s, cast=True)
        m1 = _tile_mods(mod_p, 1, tiles_per_seq=t // bm_p)
        proj = _proj(xp, norm_g[l, 1], m1, w_main, w_gate, l, bm_p)

        y_ret, s_ret_s = _retention(proj_s, state_ret[l], pos_s, bs, rq, ts, rq, F32)
        kc_s = _compress_sample(cache_cmp_kv, l, page_table, cmp_w)
        o_c, sel_t = _nsa_sample_select(proj_s, kc_s, bs, past_len, n_sel_s)
        y_nsa = _nsa_sample_attend(proj_s, o_c, sel_t, cache_sel_kv, cache_win_kv, l, page_table, past_len)
        xs, w_out_b = _outproj(xs, y_ret, y_nsa, m1s[2], (w_out, w_out), rows_s, layer=l)

        y_ret, s_ret = _retention(proj, s0_p, pos_p, bp, t, RET_CHUNK, RET_CHUNK, BF16)
        kc = _compress_prompt(proj, cmp_w, bp, t)
        y_nsa = _nsa_prompt(proj, kc, bp, t)
        xp = _outproj(xp, y_ret, y_nsa, m1[2], w_out_b, bm_p)
        projs.append(proj)
        pr.append(s_ret)

        m2 = _tile_mods(mod_s, 2, rows_per_seq=rq)
        xs, ffn_b = _ffn(xs, norm_g[l, 2], m2, ffn_w, rows_s, layer_half=(l, 1))
        m2 = _tile_mods(mod_p, 2, tiles_per_seq=t // bm)
        xp = _ffn(xp, norm_g[l, 2], m2, ffn_b, bm)

        new_kv = lambda c0: proj_s[:, c0:c0 + 2 * KV_W].reshape(bs, rq, 2, NSA_KV_HEADS, NSA_DH)[:, :ts]
        sc.append(new_kv(COL_KVC))
        ss.append(new_kv(COL_KVS))
        sw.append(jnp.concatenate([cache_win_kv[l], new_kv(COL_KVW)], axis=1)[:, ts:])
        sr.append(s_ret_s)

    pc, ps, pw = _kv_outputs(projs, bp, t, win_buf)
    y_prompt = _final_norm(xp, final_g, bm).reshape(bp, t, D_MODEL)
    y_sample = _final_norm(xs, final_g, rows_s).reshape(bs, rq, D_MODEL)[:, :ts]
    return (y_prompt, y_sample, pc, ps, pw, jnp.stack(pr),
            jnp.stack(sc), jnp.stack(ss), jnp.stack(sw), jnp.stack(sr))
```

```python
import functools

import jax
import jax.numpy as jnp
import numpy as np
from jax import lax
from jax.experimental import pallas as pl
from jax.experimental.pallas import tpu as pltpu

F32 = jnp.float32
BF16 = jnp.bfloat16

D_MODEL = 4096
DEPTH = 2
RET_HEADS = 8
RET_DK = 128
RET_DV = 256
RET_CHUNK = 128
ROPE_BASE = 10000.0
NSA_HEADS = 16
NSA_DH = 128
NSA_KV_HEADS = 4
NSA_HPG = NSA_HEADS // NSA_KV_HEADS
CMP_BLOCK = 32
CMP_STRIDE = 16
CMP_RATIO = CMP_BLOCK // CMP_STRIDE
CMP_HIDDEN = 2 * NSA_DH
SEL_BLOCK = 64
SEL_TOPK = 16
WINDOW = 512
N_SUB = 3
RMS_EPS = 1e-6
NEG_INF = -1e30
FORCE_SCORE = 1e9
KV_W = NSA_KV_HEADS * NSA_DH
ATT_SCALE = NSA_DH ** -0.5

COL_RQ = 0
COL_RK = COL_RQ + RET_HEADS * RET_DK
COL_RV = COL_RK + RET_HEADS * RET_DK
COL_RG = COL_RV + RET_HEADS * RET_DV
COL_NQ = COL_RG + RET_HEADS * RET_DV
COL_KVC = COL_NQ + NSA_HEADS * NSA_DH
COL_KVS = COL_KVC + 2 * KV_W
COL_KVW = COL_KVS + 2 * KV_W
COL_NG = COL_KVW + 2 * KV_W
N_PROJ = COL_NG + NSA_KV_HEADS * 128

LANE = 128
SAMPLE_ROWS = 8
VMEM_LIMIT_BYTES = 60 * 1024 * 1024


def _cparams(*sem):
    return pltpu.CompilerParams(dimension_semantics=sem, vmem_limit_bytes=VMEM_LIMIT_BYTES)


def _silu(x):
    return x * jax.nn.sigmoid(x)


def _dot(a, b):
    return jnp.dot(a, b, preferred_element_type=F32)


def _dot_nt(a, b):
    return lax.dot_general(a, b, (((1,), (1,)), ((), ())), preferred_element_type=F32)


def _masked_softmax(s, valid):
    s = jnp.where(valid, s, NEG_INF)
    m = jnp.max(s, axis=-1, keepdims=True)
    e = jnp.where(valid, jnp.exp(s - m), 0.0)
    return e / jnp.maximum(jnp.sum(e, axis=-1, keepdims=True), 1e-30)


def _mod_kernel(c_ref, w_ref, b_ref, o_ref):
    a = _silu(c_ref[...]).astype(BF16)
    o_ref[...] = _dot(a, w_ref[...].astype(BF16)) + b_ref[...]


def _modulation(c_all, w_ada, b_ada):
    rows = c_all.shape[0]
    n = w_ada.shape[-1]
    bn = 512
    return pl.pallas_call(
        _mod_kernel,
        grid=(DEPTH, n // bn),
        in_specs=[pl.BlockSpec((rows, D_MODEL), lambda l, j: (0, 0)),
                  pl.BlockSpec((None, D_MODEL, bn), lambda l, j: (l, 0, j)),
                  pl.BlockSpec((None, 1, bn), lambda l, j: (l, 0, j))],
        out_specs=pl.BlockSpec((None, rows, bn), lambda l, j: (l, 0, j)),
        out_shape=jax.ShapeDtypeStruct((DEPTH, rows, n), F32),
        compiler_params=_cparams("parallel", "parallel"),
    )(c_all, w_ada, b_ada.reshape(DEPTH, 1, n))


def _norm_modulate_into(h_ref, x_ref, g_ref, sh_ref, sc_ref):
    rows = x_ref.shape[0]
    per_row = sh_ref.shape[0] != 1
    step = min(rows, 128)
    if not per_row:
        gain = g_ref[...] * (1.0 + sc_ref[...])
    for r0 in range(0, rows, step):
        x = x_ref[r0:r0 + step, :]
        if per_row:
            gain = g_ref[...] * (1.0 + sc_ref[r0:r0 + step, :])
        sh = sh_ref[r0:r0 + step, :] if per_row else sh_ref[...]
        y = x * lax.rsqrt(jnp.mean(x * x, axis=-1, keepdims=True) + RMS_EPS)
        h_ref[r0:r0 + step, :] = (y * gain + sh).astype(h_ref.dtype)


def _mod_spec(mod_rows, bm):
    return pl.BlockSpec((None, mod_rows, D_MODEL), lambda i, j: (i, 0, 0))


def _ffn_kernel(x_ref, g_ref, sh_ref, sc_ref, gt_ref, wg_ref, wu_ref, wd_ref, o_ref, *rest, n_chunk):
    j = pl.program_id(1)
    h_ref = rest[-1]
    if len(rest) > 1:
        for src, dst in zip((wg_ref, wu_ref, wd_ref), rest[:3]):
            dst[...] = src[...].astype(BF16)
        wg_ref, wu_ref, wd_ref = rest[:3]

    @pl.when(j == 0)
    def _():
        _norm_modulate_into(h_ref, x_ref, g_ref, sh_ref, sc_ref)
        o_ref[...] = jnp.zeros_like(o_ref)

    h = h_ref[...]
    a = (_silu(_dot(h, wg_ref[...])) * _dot(h, wu_ref[...])).astype(BF16)
    for n0 in range(0, D_MODEL, n_chunk):
        o_ref[:, n0:n0 + n_chunk] += _dot(a, wd_ref[:, n0:n0 + n_chunk])

    @pl.when(j == pl.num_programs(1) - 1)
    def _():
        rows = x_ref.shape[0]
        per_row = gt_ref.shape[0] != 1
        step = min(rows, 128)
        for r0 in range(0, rows, step):
            gt = gt_ref[r0:r0 + step, :] if per_row else gt_ref[...]
            o_ref[r0:r0 + step, :] = x_ref[r0:r0 + step, :] + (0.5 * gt) * o_ref[r0:r0 + step, :]


def _ffn(x, g, mods, weights, bm, layer_half=None):
    rows = x.shape[0]
    wg, wu, wd = weights
    d_ff = wg.shape[-1]
    tf = 256
    shift, scale, gate = mods
    mr = shift.shape[1]
    up_block, down_block = (D_MODEL, tf), (tf, D_MODEL)
    up_idx, down_idx = (lambda i, j: (0, j)), (lambda i, j: (j, 0))
    out_specs = [pl.BlockSpec((bm, D_MODEL), lambda i, j: (i, 0))]
    out_shape = [jax.ShapeDtypeStruct((rows, D_MODEL), F32)]
    if layer_half is None:
        w_specs = [pl.BlockSpec(up_block, up_idx), pl.BlockSpec(up_block, up_idx), pl.BlockSpec(down_block, down_idx)]
    else:
        assert rows == bm
        l, k = layer_half
        w_specs = [pl.BlockSpec((None, None) + up_block, lambda i, j: (l, k, 0, j)),
                   pl.BlockSpec((None, None) + up_block, lambda i, j: (l, k, 0, j)),
                   pl.BlockSpec((None, None) + down_block, lambda i, j: (l, k, j, 0))]
        out_specs += [pl.BlockSpec(up_block, up_idx), pl.BlockSpec(up_block, up_idx),
                      pl.BlockSpec(down_block, down_idx)]
        out_shape += [jax.ShapeDtypeStruct((D_MODEL, d_ff), BF16)] * 2 + [jax.ShapeDtypeStruct((d_ff, D_MODEL), BF16)]
    out = pl.pallas_call(
        functools.partial(_ffn_kernel, n_chunk=512),
        grid=(rows // bm, d_ff // tf),
        in_specs=[pl.BlockSpec((bm, D_MODEL), lambda i, j: (i, 0)),
                  pl.BlockSpec((1, D_MODEL), lambda i, j: (0, 0)),
                  _mod_spec(mr, bm), _mod_spec(mr, bm), _mod_spec(mr, bm)] + w_specs,
        out_specs=out_specs,
        out_shape=out_shape,
        scratch_shapes=[pltpu.VMEM((bm, D_MODEL), BF16)],
        compiler_params=_cparams("parallel", "arbitrary"),
    )(x, g.reshape(1, D_MODEL), shift, scale, gate, wg, wu, wd)
    return out[0] if layer_half is None else (out[0], tuple(out[1:]))


def _proj_kernel(x_ref, g_ref, sh_ref, sc_ref, w_ref, wgate_ref, o_ref, *rest, n_main):
    j = pl.program_id(1)
    h_ref = rest[-1]

    @pl.when(j == 0)
    def _():
        _norm_modulate_into(h_ref, x_ref, g_ref, sh_ref, sc_ref)

    @pl.when(j < n_main)
    def _():
        if len(rest) > 1:
            rest[0][...] = w_ref[...].astype(BF16)
            o_ref[...] = _dot(h_ref[...], rest[0][...])
        else:
            o_ref[...] = _dot(h_ref[...], w_ref[...])

    @pl.when(j >= n_main)
    def _():
        o_ref[...] = _dot(h_ref[...], wgate_ref[...])


def _proj(x, g, mods, w_main, w_gate, layer, bm, cast=False):
    rows = x.shape[0]
    bn = N_PROJ - COL_NG
    n_main = COL_NG // bn
    shift, scale, _ = mods
    mr = shift.shape[1]
    main_idx = lambda i, j: (0, jnp.minimum(j, n_main - 1))
    out_specs = [pl.BlockSpec((bm, bn), lambda i, j: (i, j))]
    out_shape = [jax.ShapeDtypeStruct((rows, N_PROJ), F32)]
    if cast:
        assert rows == bm
        w_spec = pl.BlockSpec((None, D_MODEL, bn), lambda i, j: (layer, 0, jnp.minimum(j, n_main - 1)))
        out_specs.append(pl.BlockSpec((D_MODEL, bn), main_idx))
        out_shape.append(jax.ShapeDtypeStruct((D_MODEL, COL_NG), BF16))
    else:
        w_spec = pl.BlockSpec((D_MODEL, bn), main_idx)
    out = pl.pallas_call(
        functools.partial(_proj_kernel, n_main=n_main),
        grid=(rows // bm, n_main + 1),
        in_specs=[pl.BlockSpec((bm, D_MODEL), lambda i, j: (i, 0)),
                  pl.BlockSpec((1, D_MODEL), lambda i, j: (0, 0)),
                  _mod_spec(mr, bm), _mod_spec(mr, bm),
                  w_spec,
                  pl.BlockSpec((None, D_MODEL, bn), lambda i, j: (layer, 0, 0))],
        out_specs=out_specs,
        out_shape=out_shape,
        scratch_shapes=[pltpu.VMEM((bm, D_MODEL), BF16)],
        compiler_params=_cparams("parallel", "arbitrary"),
    )(x, g.reshape(1, D_MODEL), shift, scale, w_main, w_gate)
    return tuple(out) if cast else out[0]


def _outproj_kernel(x_ref, yr_ref, yn_ref, gt_ref, wa_ref, wb_ref, o_ref, *cast_out):
    if cast_out:
        for src, dst in zip((wa_ref, wb_ref), cast_out):
            dst[...] = src[...].astype(BF16)
        wa_ref, wb_ref = cast_out
    y = _dot(yr_ref[...].astype(BF16), wa_ref[...]) + _dot(yn_ref[...].astype(BF16), wb_ref[...])
    o_ref[...] = x_ref[...] + gt_ref[...] * y


def _outproj(x, y_ret, y_nsa, gate, w_halves, bm, layer=None):
    rows = x.shape[0]
    half = y_ret.shape[1]
    bn = 512
    mr = gate.shape[1]
    out_specs = [pl.BlockSpec((bm, bn), lambda i, j: (i, j))]
    out_shape = [jax.ShapeDtypeStruct((rows, D_MODEL), F32)]
    if layer is None:
        w_specs = [pl.BlockSpec((half, bn), lambda i, j: (0, j))] * 2
    else:
        assert rows == bm
        w_specs = [pl.BlockSpec((None, half, bn), lambda i, j: (layer, 0, j)),
                   pl.BlockSpec((None, half, bn), lambda i, j: (layer, 1, j))]
        out_specs += [pl.BlockSpec((half, bn), lambda i, j: (0, j))] * 2
        out_shape += [jax.ShapeDtypeStruct((half, D_MODEL), BF16)] * 2
    out = pl.pallas_call(
        _outproj_kernel,
        grid=(rows // bm, D_MODEL // bn),
        in_specs=[pl.BlockSpec((bm, bn), lambda i, j: (i, j)),
                  pl.BlockSpec((bm, half), lambda i, j: (i, 0)),
                  pl.BlockSpec((bm, half), lambda i, j: (i, 0)),
                  pl.BlockSpec((None, mr, bn), lambda i, j: (i, 0, j))] + w_specs,
        out_specs=out_specs,
        out_shape=out_shape,
        compiler_params=_cparams("parallel", "arbitrary"),
    )(x, y_ret, y_nsa, gate, *w_halves)
    return out[0] if layer is None else (out[0], tuple(out[1:]))


def _final_norm_kernel(x_ref, g_ref, o_ref):
    x = x_ref[...]
    o_ref[...] = x * lax.rsqrt(jnp.mean(x * x, axis=-1, keepdims=True) + RMS_EPS) * g_ref[...]


def _final_norm(x, g, bm):
    rows = x.shape[0]
    return pl.pallas_call(
        _final_norm_kernel,
        grid=(rows // bm,),
        in_specs=[pl.BlockSpec((bm, D_MODEL), lambda i: (i, 0)),
                  pl.BlockSpec((1, D_MODEL), lambda i: (0, 0))],
        out_specs=pl.BlockSpec((bm, D_MODEL), lambda i: (i, 0)),
        out_shape=jax.ShapeDtypeStruct((rows, D_MODEL), F32),
        compiler_params=_cparams("parallel"),
    )(x, g.reshape(1, D_MODEL))


def _retention_kernel(q_ref, k_ref, v_ref, rg_ref, cos_ref, sin_ref, dm_ref, qd_ref, kd_ref, cd_ref, s0_ref,
                      y_ref, s_ref, *pad, cq):
    @pl.when(pl.program_id(1) == 0)
    def _():
        s_ref[...] = s0_ref[...]

    cos = cos_ref[...]
    sin = sin_ref[...]
    padded = cq != RET_CHUNK
    if padded:
        kp_ref, vp_ref = pad
        kp_ref[...] = jnp.zeros_like(kp_ref)
        vp_ref[...] = jnp.zeros_like(vp_ref)
    for h in range(RET_HEADS):
        q = q_ref[:, h * RET_DK:(h + 1) * RET_DK]
        k = k_ref[:, h * RET_DK:(h + 1) * RET_DK]
        q = q * cos + pltpu.roll(q, RET_DK // 2, 1) * sin
        k = (k * cos + pltpu.roll(k, RET_DK // 2, 1) * sin) * (RET_DK ** -0.5)
        v = v_ref[:, h * RET_DV:(h + 1) * RET_DV]
        if padded:
            kp_ref[0:cq, :] = k
            vp_ref[0:cq, :] = v
            k = kp_ref[...]
            v = vp_ref[...]
        qb = q.astype(BF16)
        vb = v.astype(BF16)
        att = _dot_nt(qb, k.astype(BF16)) * dm_ref[h]
        s = s_ref[h]
        o = _dot(att.astype(BF16), vb) + _dot(qb, s.astype(BF16)) * qd_ref[h]
        kd = (k * kd_ref[h]).T.astype(BF16)
        s_ref[h] = s * cd_ref[h] + _dot(kd, vb)
        o = o * lax.rsqrt(jnp.mean(o * o, axis=-1, keepdims=True) + RMS_EPS)
        y = _silu(rg_ref[:, h * RET_DV:(h + 1) * RET_DV]) * o
        y_ref[:, h * RET_DV:(h + 1) * RET_DV] = y.astype(y_ref.dtype)


def _retention_tables(pos, chunk, cq):
    half = RET_DK // 2
    inv = jnp.power(ROPE_BASE, -jnp.arange(half, dtype=F32) * 2.0 / RET_DK)
    ang = pos.astype(F32)[:, None] * inv[None, :]
    cos = jnp.cos(ang)
    sin = jnp.sin(ang)
    cos2 = jnp.concatenate([cos, cos], axis=-1)
    sin2 = jnp.concatenate([-sin, sin], axis=-1)
    log_g = jnp.log(1.0 - jnp.exp2(-5.0 - jnp.arange(RET_HEADS, dtype=F32)))
    i = jnp.arange(chunk, dtype=F32)
    diff = i[:, None] - i[None, :]
    dmask = jnp.where(diff >= 0, jnp.exp(log_g[:, None, None] * jnp.maximum(diff, 0.0)), 0.0)
    q_dec = jnp.exp(log_g[None, :] * (i[:, None] + 1.0))
    k_dec = jnp.exp(log_g[None, :] * (chunk - 1.0 - i[:, None]))
    c_dec = jnp.exp(log_g * chunk)
    dm = jnp.zeros((RET_HEADS, cq, RET_CHUNK), F32).at[:, :chunk, :chunk].set(dmask)
    qd = jnp.zeros((RET_HEADS, cq, RET_DV), F32).at[:, :chunk, :].set(
        jnp.broadcast_to(q_dec.T[:, :, None], (RET_HEADS, chunk, RET_DV)))
    kd = jnp.zeros((RET_HEADS, RET_CHUNK, RET_DK), F32).at[:, :chunk, :].set(
        jnp.broadcast_to(k_dec.T[:, :, None], (RET_HEADS, chunk, RET_DK)))
    cd = jnp.broadcast_to(c_dec[:, None, None], (RET_HEADS, 1, RET_DV))
    return cos2, sin2, dm, qd, kd, cd


def _retention(proj, s0, pos, batch, t_rows, chunk, cq, out_dtype):
    nc = t_rows // cq
    cos2, sin2, dm, qd, kd, cd = _retention_tables(pos, chunk, cq)
    hk = RET_HEADS * RET_DK
    hv = RET_HEADS * RET_DV
    row = lambda b, c: b * nc + c
    scratch = [] if cq == RET_CHUNK else [pltpu.VMEM((RET_CHUNK, RET_DK), F32), pltpu.VMEM((RET_CHUNK, RET_DV), F32)]
    full3 = lambda shape: pl.BlockSpec(shape, lambda b, c: (0, 0, 0))
    return pl.pallas_call(
        functools.partial(_retention_kernel, cq=cq),
        grid=(batch, nc),
        in_specs=[pl.BlockSpec((cq, hk), lambda b, c: (row(b, c), COL_RQ // hk)),
                  pl.BlockSpec((cq, hk), lambda b, c: (row(b, c), COL_RK // hk)),
                  pl.BlockSpec((cq, hv), lambda b, c: (row(b, c), COL_RV // hv)),
                  pl.BlockSpec((cq, hv), lambda b, c: (row(b, c), COL_RG // hv)),
                  pl.BlockSpec((cq, RET_DK), lambda b, c: (c, 0)),
                  pl.BlockSpec((cq, RET_DK), lambda b, c: (c, 0)),
                  full3(dm.shape), full3(qd.shape), full3(kd.shape), full3(cd.shape),
                  pl.BlockSpec((None, RET_HEADS, RET_DK, RET_DV), lambda b, c: (b, 0, 0, 0))],
        out_specs=[pl.BlockSpec((cq, hv), lambda b, c: (row(b, c), 0)),
                   pl.BlockSpec((None, RET_HEADS, RET_DK, RET_DV), lambda b, c: (b, 0, 0, 0))],
        out_shape=[jax.ShapeDtypeStruct((batch * t_rows, hv), out_dtype),
                   jax.ShapeDtypeStruct((batch, RET_HEADS, RET_DK, RET_DV), F32)],
        scratch_shapes=scratch,
        compiler_params=_cparams("parallel", "arbitrary"),
    )(proj, proj, proj, proj, cos2, sin2, dm, qd, kd, cd, s0)


KV_PER_ROW = 2 * NSA_KV_HEADS


def _kv_rows(ref, r0, rows, c, g, step=1):
    if ref.shape[1] == NSA_DH:
        return ref[pl.ds(r0 * KV_PER_ROW + c * NSA_KV_HEADS + g, rows, stride=step * KV_PER_ROW), :]
    assert step == 1
    col = (c * NSA_KV_HEADS + g) * NSA_DH
    return ref[r0:r0 + rows, col:col + NSA_DH]


def _compress_kernel(*refs, n_src):
    srcs = refs[:n_src]
    perm_ref, w1_ref, pe_ref, b1_ref, w2_ref, o_ref, x_ref, prev_ref = refs[n_src:]
    cache_view = srcs[0].shape[1] == NSA_DH
    src_tokens = srcs[0].shape[0] // (KV_PER_ROW if cache_view else 1)
    s_src = src_tokens // CMP_STRIDE
    s_tot = n_src * s_src
    rows = NSA_KV_HEADS * s_tot

    @pl.when(pl.program_id(1) == 0)
    def _():
        prev_ref[...] = jnp.zeros_like(prev_ref)

    for si, src in enumerate(srcs):
        for c in range(2):
            for g in range(NSA_KV_HEADS):
                r0 = g * s_tot + si * s_src
                if cache_view:
                    for s in range(CMP_STRIDE):
                        x_ref[c, r0:r0 + s_src, s * NSA_DH:(s + 1) * NSA_DH] = _kv_rows(src, s, s_src, c, g, CMP_STRIDE)
                    continue
                grp = perm_ref.shape[0]
                segs = grp // CMP_STRIDE
                for gi in range(src_tokens // grp):
                    blk = _dot(perm_ref[...], _kv_rows(src, gi * grp, grp, c, g).astype(BF16))
                    for s in range(CMP_STRIDE):
                        x_ref[c, r0 + gi * segs:r0 + (gi + 1) * segs, s * NSA_DH:(s + 1) * NSA_DH] = (
                            blk[s * segs:(s + 1) * segs, :])

    first = (lax.broadcasted_iota(jnp.int32, (rows, CMP_HIDDEN), 0) & (s_tot - 1)) == 0
    for c in range(2):
        w1 = w1_ref[c]
        part = _dot(x_ref[c].astype(BF16), w1)
        pe_part = _dot(pe_ref[c], w1)
        bias = b1_ref[c] + pe_part[0:1, :CMP_HIDDEN] + pe_part[1:2, CMP_HIDDEN:]
        p0 = part[:, :CMP_HIDDEN]
        p1 = part[:, CMP_HIDDEN:]
        shifted = jnp.where(first, pltpu.roll(prev_ref[c], rows - s_tot + 1, 0), pltpu.roll(p0, 1, 0))
        prev_ref[c] = p0
        kc = _dot(_silu(bias + shifted + p1).astype(BF16), w2_ref[c])
        for g in range(NSA_KV_HEADS):
            col = (c * NSA_KV_HEADS + g) * NSA_DH
            o_ref[:, col:col + NSA_DH] = kc[g * s_tot:(g + 1) * s_tot].astype(o_ref.dtype)


def _compress_weights(cmp_pe, cmp_w1, cmp_b1, cmp_w2):
    w1 = cmp_w1.reshape(2, CMP_RATIO, CMP_STRIDE, NSA_DH, CMP_HIDDEN).transpose(0, 2, 3, 1, 4)
    w1 = w1.reshape(2, CMP_STRIDE * NSA_DH, CMP_RATIO * CMP_HIDDEN).astype(BF16)
    pe = cmp_pe.reshape(2, CMP_RATIO, CMP_STRIDE * NSA_DH)
    pe = jnp.pad(pe, [(0, 0), (0, 8 - CMP_RATIO), (0, 0)]).astype(BF16)
    return _segment_permutation(), w1, pe, cmp_b1.reshape(2, 1, CMP_HIDDEN), cmp_w2.astype(BF16)


COMPRESS_GROUP_ROWS = 128


def _segment_permutation():
    segs = COMPRESS_GROUP_ROWS // CMP_STRIDE
    r = np.arange(COMPRESS_GROUP_ROWS)
    src = (r % segs) * CMP_STRIDE + r // segs
    return jnp.asarray(src[:, None] == np.arange(COMPRESS_GROUP_ROWS)[None, :], BF16)


def _compress_specs(idx):
    kdim = CMP_STRIDE * NSA_DH
    return [pl.BlockSpec((COMPRESS_GROUP_ROWS, COMPRESS_GROUP_ROWS), idx(lambda *_: (0, 0))),
            pl.BlockSpec((2, kdim, CMP_RATIO * CMP_HIDDEN), idx(lambda *_: (0, 0, 0))),
            pl.BlockSpec((2, 8, kdim), idx(lambda *_: (0, 0, 0))),
            pl.BlockSpec((2, 1, CMP_HIDDEN), idx(lambda *_: (0, 0, 0))),
            pl.BlockSpec((2, CMP_HIDDEN, NSA_DH), idx(lambda *_: (0, 0, 0)))]


def _compress_scratch(s_tot):
    rows = NSA_KV_HEADS * s_tot
    return [pltpu.VMEM((2, rows, CMP_STRIDE * NSA_DH), F32), pltpu.VMEM((2, rows, CMP_HIDDEN), F32)]


def _compress_prompt(proj, cmp_w, batch, t):
    segs = t // CMP_STRIDE
    return pl.pallas_call(
        functools.partial(_compress_kernel, n_src=1),
        grid=(batch, 1),
        in_specs=[pl.BlockSpec((t, 2 * KV_W), lambda b, i: (b, COL_KVC // (2 * KV_W)))]
        + _compress_specs(lambda f: f),
        out_specs=pl.BlockSpec((None, segs, 2 * KV_W), lambda b, i: (b, 0, 0)),
        out_shape=jax.ShapeDtypeStruct((batch, segs, 2 * KV_W), BF16),
        scratch_shapes=_compress_scratch(segs),
        compiler_params=_cparams("parallel", "arbitrary"),
    )(proj, *cmp_w)


def _split_bf16(x):
    hi = x.astype(BF16)
    return hi, (x - hi.astype(F32)).astype(BF16)


def _selection_scores(psl_t, t_pos):
    j = lax.broadcasted_iota(jnp.int32, psl_t.shape, 0)
    jc = t_pos // SEL_BLOCK
    forced = (j == 0) | (j == jc) | (j == jc - 1)
    return jnp.where(j > jc, NEG_INF, jnp.where(forced, FORCE_SCORE, psl_t))


def _beats(row, jp, score, j):
    return jnp.where(row > score, 1.0, jnp.where(row == score, jnp.where(j > jp, 1.0, 0.0), 0.0))


def _nsa_prompt_kernel(q_ref, gt_ref, kck_ref, kcv_ref, ks_ref, vs_ref, kw_ref, vw_ref, at_ref, e_ref, cb_ref, wb_ref,
                       o_ref, ksb, vsb, kwb, vwb, m_ref, l_ref, acc_ref, *, tq, tk, n_sel):
    qi = pl.program_id(2)
    rows = NSA_HPG * tq
    groups = range(NSA_GROUPS_PER_STEP)
    dh = lambda u: slice(u * NSA_DH, (u + 1) * NSA_DH)

    @pl.when(qi == 0)
    def _():
        for u in groups:
            ksb[u] = ks_ref[:, dh(u)].astype(BF16)
            vsb[u] = vs_ref[:, dh(u)].astype(BF16)
            kwb[u] = kw_ref[:, dh(u)].astype(BF16)
            vwb[u] = vw_ref[:, dh(u)].astype(BF16)

    def t_of(shape):
        return qi * tq + (lax.broadcasted_iota(jnp.int32, shape, 0) & (tq - 1))

    q4 = [(jnp.concatenate([q_ref[:, (u * NSA_HPG + p) * NSA_DH:(u * NSA_HPG + p + 1) * NSA_DH]
                            for p in range(NSA_HPG)], axis=0) * ATT_SCALE).astype(BF16) for u in groups]

    n_slot = kck_ref.shape[0]
    slot = lax.broadcasted_iota(jnp.int32, (rows, n_slot), 1)
    valid = (slot >= 1) & (slot * CMP_STRIDE + (CMP_BLOCK - CMP_STRIDE - 1) <= t_of((rows, n_slot)))
    p_c = [_masked_softmax(_dot_nt(q4[u], kck_ref[:, dh(u)]), valid) for u in groups]
    o_c = [_dot(p_c[u].astype(BF16), kcv_ref[:, dh(u)]) for u in groups]

    eye = jnp.where(lax.broadcasted_iota(jnp.int32, (tq, tq), 0) == lax.broadcasted_iota(jnp.int32, (tq, tq), 1),
                    1.0, 0.0).astype(BF16)
    sel_bias = []
    for u in groups:
        pg = p_c[u][0:tq]
        for p in range(1, NSA_HPG):
            pg = pg + p_c[u][p * tq:(p + 1) * tq]
        hi, lo = _split_bf16(pg)
        psl_t = _dot_nt(at_ref[...], hi) + _dot_nt(at_ref[...], lo)
        jshape = psl_t.shape
        j = lax.broadcasted_iota(jnp.int32, jshape, 0)
        score = _selection_scores(psl_t, qi * tq + lax.broadcasted_iota(jnp.int32, jshape, 1))
        rank = jnp.zeros(jshape, F32)
        for jp in range(n_sel):
            rank = rank + _beats(score[jp:jp + 1, :], jp, score, j)
        bias_t = jnp.where((rank < min(SEL_TOPK, n_sel)) & (j < n_sel), 0.0, NEG_INF)
        bias_t = jnp.concatenate([bias_t, jnp.full((e_ref.shape[1] - jshape[0], tq), NEG_INF, F32)], axis=0)
        sel_bias.append(_dot_nt(eye, bias_t.astype(BF16)).astype(BF16))

    m_ref[...] = jnp.full(m_ref.shape, NEG_INF, F32)
    l_ref[...] = jnp.zeros(l_ref.shape, F32)
    acc_ref[...] = jnp.zeros(acc_ref.shape, F32)

    def chunk(c, causal_bias):
        k0 = pl.multiple_of(c * tk, tk)
        e_c = e_ref[c]
        keys = [ksb[u, pl.ds(k0, tk), :] for u in groups]
        vals = [vsb[u, pl.ds(k0, tk), :] for u in groups]
        old = [(m_ref[u], l_ref[u], acc_ref[u]) for u in groups]
        new = []
        for u in groups:
            bias = _dot(sel_bias[u], e_c)
            if causal_bias is not None:
                bias = bias + causal_bias
            s = _dot_nt(q4[u], keys[u]) + jnp.concatenate([bias] * NSA_HPG, axis=0)
            m_old, l_old, acc_old = old[u]
            m_new = jnp.maximum(m_old, jnp.max(s, axis=-1, keepdims=True))
            alpha = jnp.exp(m_old - m_new)
            e = jnp.exp(s - m_new)
            new.append((m_new, l_old * alpha + jnp.sum(e, axis=-1, keepdims=True),
                        acc_old * alpha + _dot(e.astype(BF16), vals[u])))
        for u in groups:
            m_ref[u], l_ref[u], acc_ref[u] = new[u]

    n_full = (qi * tq) // tk

    def full_chunk(c, carry):
        chunk(c, None)
        return carry

    lax.fori_loop(0, n_full, full_chunk, 0)
    chunk(n_full, cb_ref[qi % (tk // tq)])

    n_back = WINDOW // tq
    w_len = wb_ref.shape[2]
    qw_i = jnp.minimum(qi, n_back)
    w0 = pl.multiple_of((qi - qw_i) * tq, tq)
    w_bias = jnp.concatenate([wb_ref[qw_i]] * NSA_HPG, axis=0)
    gt = jax.nn.sigmoid(gt_ref[...])
    for u in groups:
        o_s = acc_ref[u] / jnp.maximum(l_ref[u], 1e-30)
        s_w = _dot_nt(q4[u], kwb[u, pl.ds(w0, w_len), :]) + w_bias
        e_w = jnp.exp(s_w - jnp.max(s_w, axis=-1, keepdims=True))
        p_w = e_w / jnp.maximum(jnp.sum(e_w, axis=-1, keepdims=True), 1e-30)
        o_w = _dot(p_w.astype(BF16), vwb[u, pl.ds(w0, w_len), :])
        for p in range(NSA_HPG):
            r = slice(p * tq, (p + 1) * tq)
            gc = u * LANE + p
            y = (gt[:, gc:gc + 1] * o_c[u][r] + gt[:, gc + NSA_HPG:gc + NSA_HPG + 1] * o_s[r]
                 + gt[:, gc + 2 * NSA_HPG:gc + 2 * NSA_HPG + 1] * o_w[r])
            col = (u * NSA_HPG + p) * NSA_DH
            o_ref[:, col:col + NSA_DH] = y.astype(o_ref.dtype)


def _selection_sum_matrix(n_rows, n_slot, n_sel):
    ratio = SEL_BLOCK // CMP_STRIDE
    j = np.arange(n_rows)[:, None]
    m = np.arange(n_slot)[None, :]
    a = (j < n_sel) & (m >= ratio * j) & (m <= ratio * j + ratio + CMP_RATIO - 2)
    return jnp.asarray(a, BF16)


def _block_expand_matrix(n_chunks, n_rows, tk, blocks_per_chunk):
    c = np.arange(n_chunks)[:, None, None]
    j = np.arange(n_rows)[None, :, None]
    k = np.arange(tk)[None, None, :]
    return jnp.asarray((c * blocks_per_chunk + k // SEL_BLOCK) == j, BF16)


NSA_GROUPS_PER_STEP = 1


def _causal_bias_table(tq, tk):
    d = np.arange(tk // tq)[:, None, None]
    r = np.arange(tq)[None, :, None]
    k = np.arange(tk)[None, None, :]
    return jnp.asarray(np.where(k <= d * tq + r, 0.0, NEG_INF), F32)


def _window_bias_table(tq, n_back, w_len):
    d = np.arange(n_back + 1)[:, None, None]
    r = np.arange(tq)[None, :, None]
    k = np.arange(w_len)[None, None, :]
    rel = k - d * tq - r
    return jnp.asarray(np.where((rel <= 0) & (rel > -WINDOW), 0.0, NEG_INF), F32)


def _nsa_prompt(proj, kc, batch, t):
    tq, tk = 128, 512
    nq = t // tq
    n_sel = t // SEL_BLOCK
    n_slot = kc.shape[1]
    n_back = WINDOW // tq
    a_t = _selection_sum_matrix(-(-n_sel // 8) * 8, n_slot, n_sel)
    e = _block_expand_matrix(t // tk, LANE, tk, tk // SEL_BLOCK)
    cb = _causal_bias_table(tq, tk)
    wb = _window_bias_table(tq, n_back, min((n_back + 1) * tq, t))
    n_u = NSA_GROUPS_PER_STEP
    qw = n_u * NSA_HPG * NSA_DH
    kw = n_u * NSA_DH
    kv = lambda base, c: pl.BlockSpec((t, kw), lambda b, g, qi: (b, (base + c * KV_W) // kw + g))
    return pl.pallas_call(
        functools.partial(_nsa_prompt_kernel, tq=tq, tk=tk, n_sel=n_sel),
        grid=(batch, NSA_KV_HEADS // n_u, nq),
        in_specs=[pl.BlockSpec((tq, qw), lambda b, g, qi: (b * nq + qi, COL_NQ // qw + g)),
                  pl.BlockSpec((tq, n_u * LANE), lambda b, g, qi: (b * nq + qi, COL_NG // (n_u * LANE) + g)),
                  pl.BlockSpec((None, n_slot, kw), lambda b, g, qi: (b, 0, g)),
                  pl.BlockSpec((None, n_slot, kw), lambda b, g, qi: (b, 0, KV_W // kw + g)),
                  kv(COL_KVS, 0), kv(COL_KVS, 1), kv(COL_KVW, 0), kv(COL_KVW, 1),
                  pl.BlockSpec(a_t.shape, lambda b, g, qi: (0, 0)),
                  pl.BlockSpec(e.shape, lambda b, g, qi: (0, 0, 0)),
                  pl.BlockSpec(cb.shape, lambda b, g, qi: (0, 0, 0)),
                  pl.BlockSpec(wb.shape, lambda b, g, qi: (0, 0, 0))],
        out_specs=pl.BlockSpec((tq, qw), lambda b, g, qi: (b * nq + qi, g)),
        out_shape=jax.ShapeDtypeStruct((batch * t, NSA_HEADS * NSA_DH), BF16),
        scratch_shapes=[pltpu.VMEM((n_u, t, NSA_DH), BF16)] * 4
        + [pltpu.VMEM((n_u, NSA_HPG * tq, 1), F32), pltpu.VMEM((n_u, NSA_HPG * tq, 1), F32),
           pltpu.VMEM((n_u, NSA_HPG * tq, NSA_DH), F32)],
        compiler_params=_cparams("parallel", "parallel", "arbitrary"),
    )(proj, proj, kc, kc, proj, proj, proj, proj, a_t, e, cb, wb)


PAGES_PER_STEP = 16


def _cache_view(cache):
    return cache.reshape(cache.shape[:-4] + (cache.shape[-4] * KV_PER_ROW, NSA_DH))


def _page_specs(layer, page_rows):
    def spec(k):
        return pl.BlockSpec((None, None, page_rows * KV_PER_ROW, NSA_DH),
                            lambda b, i, pt: (layer, pt[b, i * PAGES_PER_STEP + k], 0, 0))
    return [spec(k) for k in range(PAGES_PER_STEP)]


def _compress_sample_kernel(pt_ref, *refs):
    _compress_kernel(*refs, n_src=PAGES_PER_STEP)


def _compress_sample(pool, layer, page_table, cmp_w):
    batch, n_pages = page_table.shape
    page_rows = pool.shape[2]
    segs = page_rows // CMP_STRIDE
    s_tot = PAGES_PER_STEP * segs
    steps = n_pages // PAGES_PER_STEP
    grid_spec = pltpu.PrefetchScalarGridSpec(
        num_scalar_prefetch=1, grid=(batch, steps),
        in_specs=_page_specs(layer, page_rows) + _compress_specs(lambda f: f),
        out_specs=pl.BlockSpec((None, s_tot, 2 * KV_W), lambda b, i, pt: (b, i, 0)),
        scratch_shapes=_compress_scratch(s_tot))
    return pl.pallas_call(
        _compress_sample_kernel,
        grid_spec=grid_spec,
        out_shape=jax.ShapeDtypeStruct((batch, steps * s_tot, 2 * KV_W), BF16),
        compiler_params=_cparams("parallel", "arbitrary"),
    )(page_table, *([_cache_view(pool)] * PAGES_PER_STEP), *cmp_w)


def _stack_heads(q_ref, g):
    return jnp.concatenate([q_ref[:, (g * NSA_HPG + p) * NSA_DH:(g * NSA_HPG + p + 1) * NSA_DH]
                            for p in range(NSA_HPG)], axis=0).astype(BF16)


def _nsa_sample_select_kernel(q_ref, kck_ref, kcv_ref, at_ref, oc_ref, selt_ref, pg_ref, sc_ref, rank_ref, *,
                              past_len, n_sel):
    rq = SAMPLE_ROWS
    rows = NSA_HPG * rq
    n_slot = kck_ref.shape[0]
    pg_ref[...] = jnp.zeros_like(pg_ref)
    slot = lax.broadcasted_iota(jnp.int32, (rows, n_slot), 1)
    t_q = past_len + (lax.broadcasted_iota(jnp.int32, (rows, n_slot), 0) & (rq - 1))
    valid = (slot >= 1) & (slot * CMP_STRIDE + (CMP_BLOCK - CMP_STRIDE - 1) <= t_q)
    for g in range(NSA_KV_HEADS):
        q4 = _stack_heads(q_ref, g)
        p_c = _masked_softmax(_dot_nt(q4, kck_ref[:, g * NSA_DH:(g + 1) * NSA_DH]) * ATT_SCALE, valid)
        o_c = _dot(p_c.astype(BF16), kcv_ref[:, g * NSA_DH:(g + 1) * NSA_DH])
        pg = p_c[0:rq]
        for p in range(NSA_HPG):
            col = (g * NSA_HPG + p) * NSA_DH
            oc_ref[:, col:col + NSA_DH] = o_c[p * rq:(p + 1) * rq]
            if p:
                pg = pg + p_c[p * rq:(p + 1) * rq]
        pg_ref[g * rq:(g + 1) * rq, :] = pg
    hi, lo = _split_bf16(pg_ref[...])
    psl_t = _dot_nt(at_ref[...], hi) + _dot_nt(at_ref[...], lo)
    jshape = psl_t.shape
    t_pos = past_len + (lax.broadcasted_iota(jnp.int32, jshape, 1) & (rq - 1))
    sc_ref[...] = _selection_scores(psl_t, t_pos)
    rank_ref[...] = jnp.zeros_like(rank_ref)

    def body(jp, carry):
        j = lax.broadcasted_iota(jnp.int32, jshape, 0)
        rank_ref[...] += _beats(sc_ref[pl.ds(jp, 1), :], jp, sc_ref[...], j)
        return carry

    lax.fori_loop(0, n_sel, body, 0)
    selt_ref[...] = jnp.where(rank_ref[...] < min(SEL_TOPK, n_sel), 1.0, 0.0).astype(selt_ref.dtype)


SEL_ROWS_SAMPLE = 384


def _nsa_sample_select(proj, kc, batch, past_len, n_sel):
    n_slot = kc.shape[1]
    a_t = _selection_sum_matrix(SEL_ROWS_SAMPLE, n_slot, n_sel)
    qw = NSA_HEADS * NSA_DH
    return pl.pallas_call(
        functools.partial(_nsa_sample_select_kernel, past_len=past_len, n_sel=n_sel),
        grid=(batch,),
        in_specs=[pl.BlockSpec((SAMPLE_ROWS, qw), lambda b: (b, COL_NQ // qw)),
                  pl.BlockSpec((None, n_slot, KV_W), lambda b: (b, 0, 0)),
                  pl.BlockSpec((None, n_slot, KV_W), lambda b: (b, 0, 1)),
                  pl.BlockSpec(a_t.shape, lambda b: (0, 0))],
        out_specs=[pl.BlockSpec((SAMPLE_ROWS, qw), lambda b: (b, 0)),
                   pl.BlockSpec((None, SEL_ROWS_SAMPLE, LANE), lambda b: (b, 0, 0))],
        out_shape=[jax.ShapeDtypeStruct((batch * SAMPLE_ROWS, qw), F32),
                   jax.ShapeDtypeStruct((batch, SEL_ROWS_SAMPLE, LANE), BF16)],
        scratch_shapes=[pltpu.VMEM((LANE, n_slot), F32), pltpu.VMEM((SEL_ROWS_SAMPLE, LANE), F32),
                        pltpu.VMEM((SEL_ROWS_SAMPLE, LANE), F32)],
        compiler_params=_cparams("parallel"),
    )(proj, kc, kc, a_t)


def _nsa_sample_attend_kernel(pt_ref, *refs, past_len, page_rows):
    pages = refs[:PAGES_PER_STEP]
    (q_ref, selt_ref, e_ref, ksn_ref, kwn_ref, win_ref, gt_ref, oc_ref, o_ref,
     m_ref, l_ref, acc_ref, pad_ref) = refs[PAGES_PER_STEP:]
    i = pl.program_id(1)
    rq = SAMPLE_ROWS
    rows = NSA_HPG * rq
    blocks_per_step = PAGES_PER_STEP * page_rows // SEL_BLOCK

    @pl.when(i == 0)
    def _():
        m_ref[...] = jnp.full(m_ref.shape, NEG_INF, F32)
        l_ref[...] = jnp.zeros(l_ref.shape, F32)
        acc_ref[...] = jnp.zeros(acc_ref.shape, F32)

    def t_of(shape):
        return past_len + (lax.broadcasted_iota(jnp.int32, shape, 0) & (rq - 1))

    all_groups = range(NSA_KV_HEADS)
    q4 = [_stack_heads(q_ref, g) for g in all_groups]

    def per_head_rows(x, g):
        return jnp.concatenate([x[g * rq:(g + 1) * rq]] * NSA_HPG, axis=0)

    def online_update(keys, picked, values):
        s = jnp.concatenate([_dot_nt(q4[g], keys[g]) for g in all_groups], axis=0) * ATT_SCALE
        ok = jnp.concatenate([per_head_rows(picked, g) for g in all_groups], axis=0) > 0.5
        s = jnp.where(ok, s, NEG_INF)
        m_old = m_ref[...]
        m_new = jnp.maximum(m_old, jnp.max(s, axis=-1, keepdims=True))
        alpha = jnp.exp(m_old - m_new)
        e = jnp.where(ok, jnp.exp(s - m_new), 0.0)
        l_ref[...] = l_ref[...] * alpha + jnp.sum(e, axis=-1, keepdims=True)
        eb = e.astype(BF16)
        pv = jnp.concatenate([_dot(eb[g * rows:(g + 1) * rows], values[g]) for g in all_groups], axis=0)
        acc_ref[...] = acc_ref[...] * alpha + pv
        m_ref[...] = m_new

    eye = jnp.where(lax.broadcasted_iota(jnp.int32, (LANE, LANE), 0) == lax.broadcasted_iota(jnp.int32, (LANE, LANE), 1),
                    1.0, 0.0).astype(BF16)

    j0 = pl.multiple_of(i * blocks_per_step, blocks_per_step)
    sel = _dot_nt(eye, selt_ref[pl.ds(j0, LANE), :]).astype(BF16)
    picked = _dot(sel, e_ref[...])
    online_update(
        [jnp.concatenate([_kv_rows(pg, 0, page_rows, 0, g) for pg in pages], axis=0).astype(BF16) for g in all_groups],
        picked,
        [jnp.concatenate([_kv_rows(pg, 0, page_rows, 1, g) for pg in pages], axis=0).astype(BF16) for g in all_groups])

    @pl.when(i == pl.num_programs(1) - 1)
    def _():
        n_past_blocks = past_len // SEL_BLOCK
        pad_ref[...] = jnp.zeros_like(pad_ref)
        pad_ref[0:rq, :] = ksn_ref[...]
        sel_new = _dot_nt(eye, selt_ref[n_past_blocks:n_past_blocks + LANE, :])
        kpos = past_len + lax.broadcasted_iota(jnp.int32, (LANE, LANE), 1)
        causal = kpos <= t_of((LANE, LANE))
        online_update([_kv_rows(pad_ref, 0, LANE, 0, g).astype(BF16) for g in all_groups],
                      jnp.where(causal, sel_new[:, 0:1], 0.0),
                      [_kv_rows(pad_ref, 0, LANE, 1, g).astype(BF16) for g in all_groups])
        o_s_all = acc_ref[...] / jnp.maximum(l_ref[...], 1e-30)
        o_s = [o_s_all[g * rows:(g + 1) * rows] for g in all_groups]

        pad_ref[0:rq, :] = kwn_ref[...]
        n_buf = win_ref.shape[0] // KV_PER_ROW
        kposw = past_len - n_buf + lax.broadcasted_iota(jnp.int32, (rows, n_buf + LANE), 1)
        tw = t_of((rows, n_buf + LANE))
        okw = jnp.where(kposw <= tw, jnp.where(kposw > tw - WINDOW, 1.0, 0.0), 0.0) > 0.5
        gt = jax.nn.sigmoid(gt_ref[...])
        for g in range(NSA_KV_HEADS):
            kw = jnp.concatenate([_kv_rows(win_ref, 0, n_buf, 0, g), _kv_rows(pad_ref, 0, LANE, 0, g)],
                                 axis=0).astype(BF16)
            vw = jnp.concatenate([_kv_rows(win_ref, 0, n_buf, 1, g), _kv_rows(pad_ref, 0, LANE, 1, g)],
                                 axis=0).astype(BF16)
            p_w = _masked_softmax(_dot_nt(_stack_heads(q_ref, g), kw) * ATT_SCALE, okw)
            o_w = _dot(p_w.astype(BF16), vw)
            for p in range(NSA_HPG):
                col = (g * NSA_HPG + p) * NSA_DH
                r = slice(p * rq, (p + 1) * rq)
                gc = g * LANE + p
                y = (gt[:, gc:gc + 1] * oc_ref[:, col:col + NSA_DH]
                     + gt[:, gc + NSA_HPG:gc + NSA_HPG + 1] * o_s[g][r]
                     + gt[:, gc + 2 * NSA_HPG:gc + 2 * NSA_HPG + 1] * o_w[r])
                o_ref[:, col:col + NSA_DH] = y


def _nsa_sample_attend(proj, o_c, sel_t, sel_pool, win_kv, layer, page_table, past_len):
    batch, n_pages = page_table.shape
    page_rows = sel_pool.shape[2]
    steps = n_pages // PAGES_PER_STEP
    keys = PAGES_PER_STEP * page_rows
    e = _block_expand_matrix(1, LANE, keys, keys // SEL_BLOCK)[0]
    n_buf = win_kv.shape[2]
    qw = NSA_HEADS * NSA_DH
    rq = SAMPLE_ROWS
    grid_spec = pltpu.PrefetchScalarGridSpec(
        num_scalar_prefetch=1, grid=(batch, steps),
        in_specs=_page_specs(layer, page_rows) + [
            pl.BlockSpec((rq, qw), lambda b, i, pt: (b, COL_NQ // qw)),
            pl.BlockSpec((None, SEL_ROWS_SAMPLE, LANE), lambda b, i, pt: (b, 0, 0)),
            pl.BlockSpec(e.shape, lambda b, i, pt: (0, 0)),
            pl.BlockSpec((rq, 2 * KV_W), lambda b, i, pt: (b, COL_KVS // (2 * KV_W))),
            pl.BlockSpec((rq, 2 * KV_W), lambda b, i, pt: (b, COL_KVW // (2 * KV_W))),
            pl.BlockSpec((None, None, n_buf * KV_PER_ROW, NSA_DH), lambda b, i, pt: (layer, b, 0, 0)),
            pl.BlockSpec((rq, NSA_KV_HEADS * LANE), lambda b, i, pt: (b, COL_NG // (NSA_KV_HEADS * LANE))),
            pl.BlockSpec((rq, qw), lambda b, i, pt: (b, 0))],
        out_specs=pl.BlockSpec((rq, qw), lambda b, i, pt: (b, 0)),
        scratch_shapes=[pltpu.VMEM((NSA_HEADS * rq, 1), F32),
                        pltpu.VMEM((NSA_HEADS * rq, 1), F32),
                        pltpu.VMEM((NSA_HEADS * rq, NSA_DH), F32),
                        pltpu.VMEM((LANE, 2 * KV_W), F32)])
    return pl.pallas_call(
        functools.partial(_nsa_sample_attend_kernel, past_len=past_len, page_rows=page_rows),
        grid_spec=grid_spec,
        out_shape=jax.ShapeDtypeStruct((batch * rq, qw), F32),
        compiler_params=_cparams("parallel", "arbitrary"),
    )(page_table, *([_cache_view(sel_pool)] * PAGES_PER_STEP), proj, sel_t, e, proj, proj, _cache_view(win_kv), proj, o_c)


def _prep_gate_columns(w_in):
    ng = w_in[:, :, COL_NG:COL_NG + 3 * NSA_HEADS].reshape(DEPTH, D_MODEL, 3, NSA_KV_HEADS, NSA_HPG)
    ng = ng.transpose(0, 1, 3, 2, 4).reshape(DEPTH, D_MODEL, NSA_KV_HEADS, 3 * NSA_HPG)
    ng = jnp.pad(ng, [(0, 0), (0, 0), (0, 0), (0, LANE - 3 * NSA_HPG)])
    return ng.reshape(DEPTH, D_MODEL, NSA_KV_HEADS * LANE).astype(BF16)


def _kv_outputs_kernel(*refs, n_tiles, n_win):
    srcs, (oc_ref, os_ref, ow_ref) = refs[:-3], refs[-3:]
    layer = pl.program_id(0)
    i = pl.program_id(2)

    def relayout(src, dst):
        for c in range(2):
            for g in range(NSA_KV_HEADS):
                dst[:, c, g, :] = _kv_rows(src, 0, src.shape[0], c, g)

    for l in range(DEPTH):
        @pl.when(layer == l)
        def _(l=l):
            relayout(srcs[3 * l], oc_ref)
            relayout(srcs[3 * l + 1], os_ref)

            @pl.when(i >= n_tiles - n_win)
            def _():
                relayout(srcs[3 * l + 2], ow_ref)


def _kv_outputs(projs, batch, t, win_buf):
    tile = 512
    assert t % tile == 0 and win_buf % tile == 0
    n_tiles, n_win = t // tile, win_buf // tile
    w = 2 * KV_W
    src_spec = lambda col: pl.BlockSpec((tile, w), lambda l, b, i: (b * n_tiles + i, col // w))
    out_block = (None, None, tile, 2, NSA_KV_HEADS, NSA_DH)
    full = pl.BlockSpec(out_block, lambda l, b, i: (l, b, i, 0, 0, 0))
    last = pl.BlockSpec(out_block, lambda l, b, i: (l, b, jnp.maximum(i - (n_tiles - n_win), 0), 0, 0, 0))
    shape = lambda rows: jax.ShapeDtypeStruct((DEPTH, batch, rows, 2, NSA_KV_HEADS, NSA_DH), F32)
    return pl.pallas_call(
        functools.partial(_kv_outputs_kernel, n_tiles=n_tiles, n_win=n_win),
        grid=(DEPTH, batch, n_tiles),
        in_specs=[src_spec(col) for _ in range(DEPTH) for col in (COL_KVC, COL_KVS, COL_KVW)],
        out_specs=[full, full, last],
        out_shape=[shape(t), shape(t), shape(win_buf)],
        compiler_params=_cparams("parallel", "parallel", "arbitrary"),
    )(*[p for p in projs for _ in range(3)])


def _tile_mods(mod_rows, sub, tiles_per_seq=None, rows_per_seq=None):
    out = []
    for k in range(3):
        m = mod_rows[:, (sub * 3 + k) * D_MODEL:(sub * 3 + k + 1) * D_MODEL]
        if tiles_per_seq is not None:
            out.append(jnp.repeat(m, tiles_per_seq, axis=0)[:, None, :])
        else:
            out.append(jnp.repeat(m, rows_per_seq, axis=0)[None])
    return out


def kernel(x_prompt, x_sample, cache_cmp_kv, cache_sel_kv, cache_win_kv, state_ret, page_table, c_prompt, c_sample,
           norm_g, w_ada, b_ada, w_in, w_out, cmp_pe, cmp_w1, cmp_b1, cmp_w2, ffn_w_gate, ffn_w_up, ffn_w_down,
           final_g):
    bp, t, _ = x_prompt.shape
    bs, ts, _ = x_sample.shape
    win_buf = cache_win_kv.shape[2]
    n_pool, page_rows = cache_cmp_kv.shape[1:3]
    past_len = page_table.shape[1] * page_rows
    assert ts <= SAMPLE_ROWS and ts < CMP_STRIDE and past_len % CMP_STRIDE == 0
    assert page_table.shape[1] % PAGES_PER_STEP == 0 and past_len % SEL_BLOCK == 0
    n_sel_s = -(-(past_len + ts) // SEL_BLOCK)
    assert n_sel_s <= past_len // SEL_BLOCK + 1 and past_len // SEL_BLOCK + LANE <= SEL_ROWS_SAMPLE

    c_all = jnp.concatenate([c_prompt, c_sample], axis=0)
    c_all = jnp.pad(c_all, [(0, -c_all.shape[0] % 8), (0, 0)])
    mod = _modulation(c_all, w_ada, b_ada)

    bm = 512
    bm_p = 1024
    xp = x_prompt.reshape(bp * t, D_MODEL)
    pos_p = jnp.arange(t, dtype=jnp.int32)
    s0_p = jnp.zeros((bp, RET_HEADS, RET_DK, RET_DV), F32)
    rq = SAMPLE_ROWS
    rows_s = bs * rq
    xs = jnp.pad(x_sample, [(0, 0), (0, rq - ts), (0, 0)]).reshape(rows_s, D_MODEL)
    pos_s = past_len + jnp.arange(rq, dtype=jnp.int32)
    projs, pr, sc, ss, sw, sr = ([] for _ in range(6))
    ffn_w = (ffn_w_gate, ffn_w_up, ffn_w_down)
    w_gate = _prep_gate_columns(w_in)
    for l in range(DEPTH):
        cmp_w = _compress_weights(cmp_pe[l], cmp_w1[l], cmp_b1[l], cmp_w2[l])
        mod_p = mod[l, :bp]
        mod_s = mod[l, bp:bp + bs]

        m0 = _tile_mods(mod_s, 0, rows_per_seq=rq)
        xs, ffn_b = _ffn(xs, norm_g[l, 0], m0, ffn_w, rows_s, layer_half=(l, 0))
        m0 = _tile_mods(mod_p, 0, tiles_per_seq=t // bm)
        xp = _ffn(xp, norm_g[l, 0], m0, ffn_b, bm)

        m1s = _tile_mods(mod_s, 1, rows_per_seq=rq)
        proj_s, w_main = _proj(xs, norm_g[l, 1], m1s, w_in, w_gate, l, rows_s, cast=True)
        m1 = _tile_mods(mod_p, 1, tiles_per_seq=t // bm_p)
        proj = _proj(xp, norm_g[l, 1], m1, w_main, w_gate, l, bm_p)

        y_ret, s_ret_s = _retention(proj_s, state_ret[l], pos_s, bs, rq, ts, rq, F32)
        kc_s = _compress_sample(cache_cmp_kv, l, page_table, cmp_w)
        o_c, sel_t = _nsa_sample_select(proj_s, kc_s, bs, past_len, n_sel_s)
        y_nsa = _nsa_sample_attend(proj_s, o_c, sel_t, cache_sel_kv, cache_win_kv, l, page_table, past_len)
        xs, w_out_b = _outproj(xs, y_ret, y_nsa, m1s[2], (w_out, w_out), rows_s, layer=l)

        y_ret, s_ret = _retention(proj, s0_p, pos_p, bp, t, RET_CHUNK, RET_CHUNK, BF16)
        kc = _compress_prompt(proj, cmp_w, bp, t)
        y_nsa = _nsa_prompt(proj, kc, bp, t)
        xp = _outproj(xp, y_ret, y_nsa, m1[2], w_out_b, bm_p)
        projs.append(proj)
        pr.append(s_ret)

        m2 = _tile_mods(mod_s, 2, rows_per_seq=rq)
        xs, ffn_b = _ffn(xs, norm_g[l, 2], m2, ffn_w, rows_s, layer_half=(l, 1))
        m2 = _tile_mods(mod_p, 2, tiles_per_seq=t // bm)
        xp = _ffn(xp, norm_g[l, 2], m2, ffn_b, bm)

        new_kv = lambda c0: proj_s[:, c0:c0 + 2 * KV_W].reshape(bs, rq, 2, NSA_KV_HEADS, NSA_DH)[:, :ts]
        sc.append(new_kv(COL_KVC))
        ss.append(new_kv(COL_KVS))
        sw.append(jnp.concatenate([cache_win_kv[l], new_kv(COL_KVW)], axis=1)[:, ts:])
        sr.append(s_ret_s)

    pc, ps, pw = _kv_outputs(projs, bp, t, win_buf)
    y_prompt = _final_norm(xp, final_g, bm).reshape(bp, t, D_MODEL)
    y_sample = _final_norm(xs, final_g, rows_s).reshape(bs, rq, D_MODEL)[:, :ts]
    return (y_prompt, y_sample, pc, ps, pw, jnp.stack(pr),
            jnp.stack(sc), jnp.stack(ss), jnp.stack(sw), jnp.stack(sr))
```
